```python
import jax, jax.numpy as jnp
from jax import lax
import numpy as np

D_MODEL = 1024
BATCH = 4
SEQ = 8192
DEPTH = 1

MLSTM_HEADS = 4
MLSTM_DV = D_MODEL // MLSTM_HEADS
MLSTM_DQK = MLSTM_DV // 2
MLSTM_CHUNK = 64
CONV_K = 4
NSA_DH = 64
NSA_HEADS = (D_MODEL // 2) // NSA_DH
NSA_KV_GROUPS = 2
NSA_HPG = NSA_HEADS // NSA_KV_GROUPS
CMP_BLOCK = 32
CMP_STRIDE = 16
CMP_HIDDEN = 256
SLC_BLOCK = 64
SLC_TOPN = 16
WINDOW = 512
NSA_QBLOCK = 128
FFN_HIDDEN = ((8 * D_MODEL + 3 * 256 - 1) // (3 * 256)) * 256
RMS_EPS = 1e-6
NEG_INF = -1e30
SEL_BIG = 1e9

M_QK = MLSTM_HEADS * MLSTM_DQK
M_V = MLSTM_HEADS * MLSTM_DV
N_Q = NSA_HEADS * NSA_DH
N_KV = NSA_KV_GROUPS * NSA_DH
IN_SPLITS = (2 * M_QK, M_V, M_V, MLSTM_HEADS, MLSTM_HEADS, N_Q, 6 * N_KV, 3 * NSA_HEADS, D_MODEL, D_MODEL)
IN_WIDTH = sum(IN_SPLITS)

kernel_name = 'hybrid_mlstm_nsa_block'


def rms_norm(x, g):
    xf = x.astype(jnp.float32)
    y = xf * lax.rsqrt(jnp.mean(xf * xf, axis=-1, keepdims=True) + RMS_EPS)
    return (y * g.astype(jnp.float32)).astype(x.dtype)


def masked_softmax(s, mask):
    s = jnp.where(mask, s.astype(jnp.float32), NEG_INF)
    return jnp.where(mask, jax.nn.softmax(s, axis=-1), 0.0)


def causal_conv(x, w, b):
    c = x.shape[-1]
    y = lax.conv_general_dilated(x, w[:, None, :], window_strides=(1,), padding=[(CONV_K - 1, 0)],
                                 dimension_numbers=('NWC', 'WIO', 'NWC'), feature_group_count=c)
    return y + b


def mlstm_chunkwise(q, k, v, log_i, log_f):
    B, H, S, DK = q.shape
    DV = v.shape[-1]
    L = MLSTM_CHUNK
    nc = S // L

    def chunks(a):
        return jnp.moveaxis(a.reshape(B, H, nc, L, *a.shape[3:]), 2, 0)

    causal = jnp.tril(jnp.ones((L, L), dtype=bool))

    def step(carry, inp):
        C, n, m = carry
        qc, kc, vc, ic, fc = inp
        b = jnp.cumsum(fc, axis=-1)
        d_intra = jnp.where(causal, b[..., :, None] - b[..., None, :] + ic[..., None, :], -jnp.inf)
        d_inter = b + m[..., None]
        m_t = jnp.maximum(d_inter, jnp.max(d_intra, axis=-1))
        w_intra = jnp.exp(d_intra - m_t[..., None])
        w_inter = jnp.exp(d_inter - m_t)
        s = jnp.einsum('bhld,bhsd->bhls', qc, kc) * w_intra
        num = jnp.einsum('bhls,bhsv->bhlv', s, vc) + w_inter[..., None] * jnp.einsum('bhld,bhvd->bhlv', qc, C)
        nq = jnp.sum(s, axis=-1) + w_inter * jnp.einsum('bhld,bhd->bhl', qc, n)
        h = num / jnp.maximum(jnp.abs(nq), jnp.exp(-m_t))[..., None]
        b_last = b[..., -1]
        d_state = b_last[..., None] - b + ic
        m_new = jnp.maximum(b_last + m, jnp.max(d_state, axis=-1))
        w_state = jnp.exp(d_state - m_new[..., None])
        decay = jnp.exp(b_last + m - m_new)
        C_new = decay[..., None, None] * C + jnp.einsum('bhsv,bhsd->bhvd', vc * w_state[..., None], kc)
        n_new = decay[..., None] * n + jnp.einsum('bhs,bhsd->bhd', w_state, kc)
        return (C_new, n_new, m_new), h

    init = (jnp.zeros((B, H, DV, DK), jnp.float32), jnp.zeros((B, H, DK), jnp.float32), jnp.zeros((B, H), jnp.float32))
    _, h = lax.scan(step, init, (chunks(q), chunks(k), chunks(v), chunks(log_i), chunks(log_f)))
    return jnp.moveaxis(h, 0, 2).reshape(B, H, S, DV)


def mlstm_branch(qk, v, o, i_pre, f_pre, f_bias, conv_w, conv_b, norm_g):
    B, S, _ = v.shape
    qk = jax.nn.silu(causal_conv(qk, conv_w, conv_b))
    q, k = jnp.split(qk, 2, axis=-1)

    def heads(a, d):
        return a.reshape(B, S, MLSTM_HEADS, d).transpose(0, 2, 1, 3).astype(jnp.float32)

    q = heads(q, MLSTM_DQK)
    k = heads(k, MLSTM_DQK) * (MLSTM_DQK ** -0.5)
    vh = heads(v, MLSTM_DV)
    log_i = i_pre.astype(jnp.float32).transpose(0, 2, 1)
    log_f = jax.nn.log_sigmoid((f_pre + f_bias).astype(jnp.float32)).transpose(0, 2, 1)
    h = mlstm_chunkwise(q, k, vh, log_i, log_f).transpose(0, 2, 1, 3)
    h = rms_norm(h, norm_g.reshape(MLSTM_HEADS, MLSTM_DV))
    h = jax.nn.sigmoid(o.astype(jnp.float32)).reshape(B, S, MLSTM_HEADS, MLSTM_DV) * h
    return h.reshape(B, S, M_V).astype(v.dtype)


def compress_blocks(tok, pos_emb, w1, w2):
    B, S, G, dh = tok.shape
    n_cmp = (S - CMP_BLOCK) // CMP_STRIDE + 1
    idx = np.arange(n_cmp)[:, None] * CMP_STRIDE + np.arange(CMP_BLOCK)[None, :]
    blocks = tok[:, idx] + pos_emb[None, None, :, None, :]
    blocks = blocks.transpose(0, 1, 3, 2, 4).reshape(B, n_cmp, G, CMP_BLOCK * dh)
    return jax.nn.silu(blocks @ w1) @ w2


def nsa_branch(q, g_br, kv, cmp_k_pos, cmp_k_w1, cmp_k_w2, cmp_v_pos, cmp_v_w1, cmp_v_w2):
    B, S, _ = q.shape
    G, HPG, dh, Qb = NSA_KV_GROUPS, NSA_HPG, NSA_DH, NSA_QBLOCK
    k_cmp, v_cmp, k_slc, v_slc, k_win, v_win = [a.reshape(B, S, G, dh) for a in jnp.split(kv, 6, axis=-1)]
    kc = compress_blocks(k_cmp, cmp_k_pos, cmp_k_w1, cmp_k_w2)
    vc = compress_blocks(v_cmp, cmp_v_pos, cmp_v_w1, cmp_v_w2)
    n_cmp = kc.shape[1]
    cmp_end = jnp.arange(n_cmp) * CMP_STRIDE + CMP_BLOCK - 1
    n_slc = S // SLC_BLOCK
    n_sel = min(SLC_TOPN, n_slc)
    ratio, span = SLC_BLOCK // CMP_STRIDE, CMP_BLOCK // CMP_STRIDE
    offs = (np.arange(ratio)[:, None] - np.arange(span)[None, :]).reshape(-1)
    map_idx = np.arange(n_slc)[:, None] * ratio + offs[None, :]
    map_valid = (map_idx >= 0) & (map_idx < n_cmp)
    map_idx = np.clip(map_idx, 0, n_cmp - 1)
    ks_blocks = k_slc.reshape(B, n_slc, SLC_BLOCK, G, dh).transpose(0, 3, 1, 2, 4)
    vs_blocks = v_slc.reshape(B, n_slc, SLC_BLOCK, G, dh).transpose(0, 3, 1, 2, 4)
    kw_pad = jnp.pad(k_win, ((0, 0), (WINDOW, 0), (0, 0), (0, 0)))
    vw_pad = jnp.pad(v_win, ((0, 0), (WINDOW, 0), (0, 0), (0, 0)))
    bi = jnp.arange(B)[:, None, None, None]
    gi = jnp.arange(G)[None, None, :, None]
    blk = jnp.arange(n_slc)
    scale = dh ** -0.5

    def block_fn(args):
        qb, gb, q0 = args
        t = q0 + jnp.arange(Qb)
        qg = qb.reshape(B, Qb, G, HPG, dh) * scale
        s = jnp.einsum('bqghd,bcgd->bqghc', qg, kc)
        mask_c = (cmp_end[None, :] <= t[:, None])[None, :, None, None, :]
        p_c = masked_softmax(s, mask_c)
        o_cmp = jnp.einsum('bqghc,bcgd->bqghd', p_c.astype(vc.dtype), vc)
        imp = jnp.sum(p_c, axis=3)
        imp = jnp.sum(jnp.where(map_valid, imp[..., map_idx], 0.0), axis=-1)
        cur = t // SLC_BLOCK
        eligible = blk[None, :] <= cur[:, None]
        forced = (blk[None, :] == 0) | (blk[None, :] == cur[:, None]) | (blk[None, :] == cur[:, None] - 1)
        score = jnp.where(forced[None, :, None, :], SEL_BIG, jnp.where(eligible[None, :, None, :], imp, -SEL_BIG))
        _, idx = lax.top_k(score, n_sel)
        kg = ks_blocks[bi, gi, idx]
        vg = vs_blocks[bi, gi, idx]
        pos = idx[..., None] * SLC_BLOCK + jnp.arange(SLC_BLOCK)
        mask_s = (pos <= t[None, :, None, None, None]).reshape(B, Qb, G, 1, n_sel * SLC_BLOCK)
        s = jnp.einsum('bqghd,bqgnrd->bqghnr', qg, kg).reshape(B, Qb, G, HPG, n_sel * SLC_BLOCK)
        p_s = masked_softmax(s, mask_s).reshape(B, Qb, G, HPG, n_sel, SLC_BLOCK)
        o_slc = jnp.einsum('bqghnr,bqgnrd->bqghd', p_s.astype(vg.dtype), vg)
        kw = lax.dynamic_slice_in_dim(kw_pad, q0, Qb + WINDOW, axis=1)
        vw = lax.dynamic_slice_in_dim(vw_pad, q0, Qb + WINDOW, axis=1)
        kpos = q0 - WINDOW + jnp.arange(Qb + WINDOW)
        mask_w = ((kpos[None, :] <= t[:, None]) & (kpos[None, :] > t[:, None] - WINDOW) & (kpos[None, :] >= 0))[None, :, None, None, :]
        s = jnp.einsum('bqghd,bkgd->bqghk', qg, kw)
        p_w = masked_softmax(s, mask_w)
        o_win = jnp.einsum('bqghk,bkgd->bqghd', p_w.astype(vw.dtype), vw)
        gates = jax.nn.sigmoid(gb.astype(jnp.float32)).reshape(B, Qb, G, HPG, 3).astype(o_cmp.dtype)
        o = gates[..., 0:1] * o_cmp + gates[..., 1:2] * o_slc + gates[..., 2:3] * o_win
        return o.reshape(B, Qb, N_Q)

    n_qb = S // Qb
    q_blocks = q.reshape(B, n_qb, Qb, N_Q).transpose(1, 0, 2, 3)
    g_blocks = g_br.reshape(B, n_qb, Qb, 3 * NSA_HEADS).transpose(1, 0, 2, 3)
    starts = jnp.arange(n_qb, dtype=jnp.int32) * Qb
    out = lax.map(block_fn, (q_blocks, g_blocks, starts))
    return out.transpose(1, 0, 2, 3).reshape(B, S, N_Q)


def setup_inputs(seed: int = 0) -> dict:
    key = jax.random.key(seed)
    ks = jax.random.split(key, 24)

    def nrm(k, shape, scale):
        return jax.random.normal(k, shape, jnp.float32) * scale

    ph = CMP_BLOCK * NSA_DH
    return {
        'x': nrm(ks[0], (BATCH, SEQ, D_MODEL), 1.0),
        'norm1_g': 1.0 + nrm(ks[1], (DEPTH, D_MODEL), 0.02),
        'w_in': nrm(ks[2], (DEPTH, D_MODEL, IN_WIDTH), D_MODEL ** -0.5),
        'b_in': nrm(ks[3], (DEPTH, IN_WIDTH), 0.02),
        'f_bias': jnp.linspace(3.0, 6.0, MLSTM_HEADS, dtype=jnp.float32)[None, :] + nrm(ks[4], (DEPTH, MLSTM_HEADS), 0.1),
        'conv_w': nrm(ks[5], (DEPTH, CONV_K, 2 * M_QK), CONV_K ** -0.5),
        'conv_b': nrm(ks[6], (DEPTH, 2 * M_QK), 0.02),
        'mlstm_norm_g': 1.0 + nrm(ks[7], (DEPTH, M_V), 0.02),
        'cmp_k_pos': nrm(ks[8], (DEPTH, CMP_BLOCK, NSA_DH), 0.02),
        'cmp_k_w1': nrm(ks[9], (DEPTH, ph, CMP_HIDDEN), ph ** -0.5),
        'cmp_k_w2': nrm(ks[10], (DEPTH, CMP_HIDDEN, NSA_DH), CMP_HIDDEN ** -0.5),
        'cmp_v_pos': nrm(ks[11], (DEPTH, CMP_BLOCK, NSA_DH), 0.02),
        'cmp_v_w1': nrm(ks[12], (DEPTH, ph, CMP_HIDDEN), ph ** -0.5),
        'cmp_v_w2': nrm(ks[13], (DEPTH, CMP_HIDDEN, NSA_DH), CMP_HIDDEN ** -0.5),
        'w_branch_a': nrm(ks[14], (DEPTH, M_V, D_MODEL), M_V ** -0.5),
        'w_branch_b': nrm(ks[15], (DEPTH, N_Q, D_MODEL), N_Q ** -0.5),
        'w_out': nrm(ks[16], (DEPTH, D_MODEL, D_MODEL), D_MODEL ** -0.5),
        'norm2_g': 1.0 + nrm(ks[17], (DEPTH, D_MODEL), 0.02),
        'w_ffn_gate': nrm(ks[18], (DEPTH, D_MODEL, FFN_HIDDEN), D_MODEL ** -0.5),
        'w_ffn_up': nrm(ks[19], (DEPTH, D_MODEL, FFN_HIDDEN), D_MODEL ** -0.5),
        'w_ffn_down': nrm(ks[20], (DEPTH, FFN_HIDDEN, D_MODEL), FFN_HIDDEN ** -0.5),
        'norm_f_g': 1.0 + nrm(ks[21], (D_MODEL,), 0.02),
    }


def reference(x, norm1_g, w_in, b_in, f_bias, conv_w, conv_b, mlstm_norm_g, cmp_k_pos, cmp_k_w1, cmp_k_w2,
              cmp_v_pos, cmp_v_w1, cmp_v_w2, w_branch_a, w_branch_b, w_out, norm2_g, w_ffn_gate, w_ffn_up,
              w_ffn_down, norm_f_g):
    split_at = np.cumsum(IN_SPLITS)[:-1].tolist()
    for l in range(DEPTH):
        h = rms_norm(x, norm1_g[l])
        proj = h @ w_in[l] + b_in[l]
        (m_qk, m_v, m_o, m_i, m_f, n_q, n_kv, n_g, gate_a, gate_b) = jnp.split(proj, split_at, axis=-1)
        y_a = mlstm_branch(m_qk, m_v, m_o, m_i, m_f, f_bias[l], conv_w[l], conv_b[l], mlstm_norm_g[l])
        y_b = nsa_branch(n_q, n_g, n_kv, cmp_k_pos[l], cmp_k_w1[l], cmp_k_w2[l], cmp_v_pos[l], cmp_v_w1[l], cmp_v_w2[l])
        merged = jax.nn.sigmoid(gate_a) * (y_a @ w_branch_a[l]) + jax.nn.sigmoid(gate_b) * (y_b @ w_branch_b[l])
        x = x + merged @ w_out[l]
        h = rms_norm(x, norm2_g[l])
        x = x + (jax.nn.silu(h @ w_ffn_gate[l]) * (h @ w_ffn_up[l])) @ w_ffn_down[l]
    return rms_norm(x, norm_f_g)
```

```python
import functools

import numpy as np
import jax
import jax.numpy as jnp
from jax import lax
from jax.experimental import pallas as pl
from jax.experimental.pallas import tpu as pltpu

F32 = jnp.float32
BF16 = jnp.bfloat16

D_MODEL = 1024
MLSTM_HEADS = 4
MLSTM_DV = D_MODEL // MLSTM_HEADS
MLSTM_DQK = MLSTM_DV // 2
CONV_K = 4
NSA_DH = 64
NSA_HEADS = (D_MODEL // 2) // NSA_DH
NSA_KV_GROUPS = 2
NSA_HPG = NSA_HEADS // NSA_KV_GROUPS
CMP_BLOCK = 32
CMP_STRIDE = 16
CMP_HIDDEN = 256
SLC_BLOCK = 64
SLC_TOPN = 16
WINDOW = 512
FFN_HIDDEN = 2816
RMS_EPS = 1e-6
SEL_BIG = 1e9

M_QK = MLSTM_HEADS * MLSTM_DQK
M_V = MLSTM_HEADS * MLSTM_DV
N_Q = NSA_HEADS * NSA_DH
N_KV = NSA_KV_GROUPS * NSA_DH
IN_SPLITS = (2 * M_QK, M_V, M_V, MLSTM_HEADS, MLSTM_HEADS, N_Q, 6 * N_KV, 3 * NSA_HEADS, D_MODEL, D_MODEL)

LANES = 128
SLC_SLOTS = 128
MASK_BIG = float(2.0 ** 100)
NEG_BIG = -1e30
VMEM_LIMIT = 56 * 1024 * 1024

_SEG = dict(zip(("m_qk", "m_v", "m_o", "m_i", "m_f", "n_q", "n_kv", "n_g", "gate_a", "gate_b"),
                zip(np.cumsum((0,) + IN_SPLITS[:-1]).tolist(), IN_SPLITS)))
_WIDE = ("m_qk", "m_v", "m_o", "n_q", "n_kv", "gate_a", "gate_b")
_NARROW = ("m_i", "m_f", "n_g")
_NARROW_W = LANES
GATE_COL0 = 2 * MLSTM_HEADS


def _cparams(sem):
    return pltpu.CompilerParams(dimension_semantics=sem, vmem_limit_bytes=VMEM_LIMIT)


def _sigmoid(x):
    return 1.0 / (1.0 + jnp.exp(-x))


def _log_sigmoid(x):
    return jnp.minimum(x, 0.0) - jnp.log(1.0 + jnp.exp(-jnp.abs(x)))


def _rms(x, g):
    ms = jnp.mean(x * x, axis=-1, keepdims=True)
    return x * lax.rsqrt(ms + RMS_EPS) * g


def _dot_nt(a, b):
    return lax.dot_general(a, b, (((1,), (1,)), ((), ())), preferred_element_type=F32)


def _dot(a, b):
    return jnp.dot(a, b, preferred_element_type=F32)


def _inproj_kernel(x_ref, g_ref, w_ref, b_ref, *out_refs, widths):
    h = _rms(x_ref[...], g_ref[...]).astype(BF16)
    off = 0
    for o_ref, width in zip(out_refs, widths):
        acc = _dot(h, w_ref[:, off:off + width]) + b_ref[:, off:off + width]
        o_ref[...] = acc.astype(o_ref.dtype)
        off += width


def _inproj(x2d, norm_g, w_in, b_in, tm):
    T = x2d.shape[0]
    cols = [w_in[:, _SEG[n][0]:_SEG[n][0] + _SEG[n][1]] for n in _WIDE]
    bias = [b_in[_SEG[n][0]:_SEG[n][0] + _SEG[n][1]] for n in _WIDE]
    narrow_w = jnp.concatenate([w_in[:, _SEG[n][0]:_SEG[n][0] + _SEG[n][1]] for n in _NARROW], axis=1)
    narrow_b = jnp.concatenate([b_in[_SEG[n][0]:_SEG[n][0] + _SEG[n][1]] for n in _NARROW])
    pad = _NARROW_W - narrow_w.shape[1]
    cols.append(jnp.pad(narrow_w, ((0, 0), (0, pad))))
    bias.append(jnp.pad(narrow_b, (0, pad)))
    w = jnp.concatenate(cols, axis=1).astype(BF16)
    b = jnp.concatenate(bias)[None, :].astype(F32)
    widths = tuple(_SEG[n][1] for n in _WIDE) + (_NARROW_W,)
    dtypes = (BF16,) * len(_WIDE) + (F32,)
    n_tot = w.shape[1]
    return pl.pallas_call(
        functools.partial(_inproj_kernel, widths=widths),
        grid=(T // tm,),
        in_specs=[
            pl.BlockSpec((tm, D_MODEL), lambda i: (i, 0)),
            pl.BlockSpec((1, D_MODEL), lambda i: (0, 0)),
            pl.BlockSpec((D_MODEL, n_tot), lambda i: (0, 0)),
            pl.BlockSpec((1, n_tot), lambda i: (0, 0)),
        ],
        out_specs=[pl.BlockSpec((tm, wd), lambda i: (i, 0)) for wd in widths],
        out_shape=[jax.ShapeDtypeStruct((T, wd), dt) for wd, dt in zip(widths, dtypes)],
        compiler_params=_cparams(("parallel",)),
        name="inproj",
    )(x2d, norm_g[None, :], w, b)


_HALO = 16


def _conv_kernel(x_ref, halo_ref, w_ref, b_ref, q_ref, k_ref):
    i = pl.program_id(1)
    ts = x_ref.shape[0]
    x = x_ref[...].astype(F32)
    halo = jnp.where(i > 0, halo_ref[...].astype(F32), 0.0)
    xx = jnp.concatenate([halo, x], axis=0)
    acc = x * w_ref[CONV_K - 1:CONV_K, :] + b_ref[...]
    for j in range(1, CONV_K):
        shifted = pltpu.roll(xx, j, 0)[_HALO:_HALO + ts]
        acc = acc + shifted * w_ref[CONV_K - 1 - j:CONV_K - j, :]
    y = acc * _sigmoid(acc)
    q_ref[...] = y[:, :M_QK].astype(BF16)
    k_ref[...] = (y[:, M_QK:] * (MLSTM_DQK ** -0.5)).astype(BF16)


def _conv(m_qk, conv_w, conv_b, ts):
    B, S, C = m_qk.shape
    per = ts // _HALO
    return pl.pallas_call(
        _conv_kernel,
        grid=(B, S // ts),
        in_specs=[
            pl.BlockSpec((None, ts, C), lambda b, i: (b, i, 0)),
            pl.BlockSpec((None, _HALO, C), lambda b, i: (b, jnp.maximum(i * per - 1, 0), 0)),
            pl.BlockSpec((CONV_K, C), lambda b, i: (0, 0)),
            pl.BlockSpec((1, C), lambda b, i: (0, 0)),
        ],
        out_specs=[pl.BlockSpec((None, ts, M_QK), lambda b, i: (b, i, 0))] * 2,
        out_shape=[jax.ShapeDtypeStruct((B, S, M_QK), BF16)] * 2,
        compiler_params=_cparams(("parallel", "parallel")),
        name="mlstm_conv",
    )(m_qk, m_qk, conv_w, conv_b[None, :])


def _mlstm_kernel(q_ref, k_ref, v_ref, o_ref, gr_ref, gc_ref, fb_ref, ng_ref, y_ref, c_scr, n_scr, m_scr):
    hd = pl.program_id(1)
    L = q_ref.shape[0]

    @pl.when(pl.program_id(2) == 0)
    def _():
        c_scr[...] = jnp.zeros_like(c_scr)
        n_scr[...] = jnp.zeros_like(n_scr)
        m_scr[...] = jnp.zeros_like(m_scr)

    q = q_ref[...]
    k = k_ref[...]
    v = v_ref[...]
    fb = fb_ref[:, 0:1]
    i_row = gr_ref[pl.ds(hd, 1), :]
    lf_row = _log_sigmoid(gr_ref[pl.ds(MLSTM_HEADS + hd, 1), :] + fb)
    i_col = gc_ref[:, 0:1]
    lf_col = _log_sigmoid(gc_ref[:, 1:2] + fb)
    m_prev = m_scr[...]

    r_t = lax.broadcasted_iota(jnp.int32, (L, L), 0)
    r_s = lax.broadcasted_iota(jnp.int32, (L, L), 1)
    causal = r_s <= r_t
    b_col = jnp.sum(jnp.where(causal, lf_row, 0.0), axis=1, keepdims=True)
    b_row = jnp.sum(jnp.where(r_t <= r_s, lf_col, 0.0), axis=0, keepdims=True)
    b_last = jnp.sum(lf_row, axis=1, keepdims=True)

    d_intra = jnp.where(causal, b_col - b_row + i_row, NEG_BIG)
    d_inter = b_col + m_prev
    m_t = jnp.maximum(d_inter, jnp.max(d_intra, axis=1, keepdims=True))
    w_intra = jnp.exp(d_intra - m_t)
    w_inter = jnp.exp(d_inter - m_t)
    s = _dot_nt(q, k) * w_intra
    c_state = c_scr[...]
    n_state = n_scr[...]
    num = _dot(s.astype(BF16), v) + w_inter * _dot_nt(q, c_state.astype(BF16))
    nq = jnp.sum(s, axis=1, keepdims=True) + w_inter * jnp.sum(q.astype(F32) * n_state, axis=1, keepdims=True)
    h = num / jnp.maximum(jnp.abs(nq), jnp.exp(-m_t))
    h = _rms(h, ng_ref[...])
    y_ref[...] = (_sigmoid(o_ref[...].astype(F32)) * h).astype(y_ref.dtype)

    d_state_col = b_last - b_col + i_col
    d_state_row = b_last - b_row + i_row
    m_new = jnp.maximum(b_last + m_prev, jnp.max(d_state_row, axis=1, keepdims=True))
    w_state = jnp.exp(d_state_col - m_new)
    decay = jnp.exp(b_last + m_prev - m_new)
    vw_t = (v.astype(F32) * w_state).T.astype(BF16)
    c_scr[...] = decay * c_state + _dot(vw_t, k)
    n_scr[...] = decay * n_state + jnp.sum(w_state * k.astype(F32), axis=0, keepdims=True)
    m_scr[...] = m_new


def _mlstm(q, k, v, o, gates_row, gates_col, f_bias, norm_g, L):
    B, S, _ = q.shape
    H, DK, DV = MLSTM_HEADS, MLSTM_DQK, MLSTM_DV
    fb = jnp.broadcast_to(f_bias.astype(F32)[:, None, None], (H, 1, LANES))
    return pl.pallas_call(
        _mlstm_kernel,
        grid=(B, H, S // L),
        in_specs=[
            pl.BlockSpec((None, L, DK), lambda b, h, c: (b, c, h)),
            pl.BlockSpec((None, L, DK), lambda b, h, c: (b, c, h)),
            pl.BlockSpec((None, L, DV), lambda b, h, c: (b, c, h)),
            pl.BlockSpec((None, L, DV), lambda b, h, c: (b, c, h)),
            pl.BlockSpec((None, 2 * H, L), lambda b, h, c: (b, 0, c)),
            pl.BlockSpec((None, None, L, 2), lambda b, h, c: (b, h, c, 0)),
            pl.BlockSpec((None, 1, LANES), lambda b, h, c: (h, 0, 0)),
            pl.BlockSpec((1, DV), lambda b, h, c: (0, h)),
        ],
        out_specs=pl.BlockSpec((None, L, DV), lambda b, h, c: (b, c, h)),
        out_shape=jax.ShapeDtypeStruct((B, S, M_V), BF16),
        scratch_shapes=[pltpu.VMEM((DV, DK), F32), pltpu.VMEM((1, DK), F32), pltpu.VMEM((1, 1), F32)],
        compiler_params=_cparams(("parallel", "parallel", "arbitrary")),
        name="mlstm",
    )(q, k, v, o, gates_row, gates_col, fb, norm_g[None, :])


def _compress_kernel(s_ref, pos_ref, w1_ref, w2_ref, out_ref):
    n = s_ref.shape[0]
    half = w1_ref.shape[0] // 2
    strips = s_ref[...]
    w1 = w1_ref[...].astype(BF16)
    first = _dot(strips, w1[:half])
    second = _dot(strips, w1[half:])
    pos_term = _dot(jnp.broadcast_to(pos_ref[...], (8, 2 * half)).astype(BF16), w1)[0:1]
    hid = first + pltpu.roll(second, n - 1, 0) + pos_term
    hid = hid * _sigmoid(hid)
    out = _dot(hid.astype(BF16), w2_ref[...].astype(BF16))
    row = lax.broadcasted_iota(jnp.int32, out.shape, 0)
    out_ref[...] = jnp.where(row < n - 1, out, 0.0).astype(out_ref.dtype)


def _compress(strips, pos, w1, w2):
    _, B, G, n, sw = strips.shape
    return pl.pallas_call(
        _compress_kernel,
        grid=(2, B, G),
        in_specs=[
            pl.BlockSpec((None, None, None, n, sw), lambda a, b, g: (a, b, g, 0, 0)),
            pl.BlockSpec((None, 1, 2 * sw), lambda a, b, g: (a, 0, 0)),
            pl.BlockSpec((None, 2 * sw, CMP_HIDDEN), lambda a, b, g: (a, 0, 0)),
            pl.BlockSpec((None, CMP_HIDDEN, NSA_DH), lambda a, b, g: (a, 0, 0)),
        ],
        out_specs=pl.BlockSpec((None, None, None, n, NSA_DH), lambda a, b, g: (a, b, g, 0, 0)),
        out_shape=jax.ShapeDtypeStruct((2, B, G, n, NSA_DH), BF16),
        compiler_params=_cparams(("parallel", "parallel", "parallel")),
        name="nsa_compress",
    )(strips, pos, w1, w2)


def _select_kernel(q_ref, kc_ref, vc_ref, map_ref, ocmp_ref, mask_ref):
    tq = q_ref.shape[0]
    ncp = kc_ref.shape[1]
    t0 = pl.program_id(1) * tq
    q = q_ref[...]
    t_col = t0 + lax.broadcasted_iota(jnp.int32, (tq, 1), 0)
    c_row = lax.broadcasted_iota(jnp.int32, (1, ncp), 1)
    visible = (c_row * CMP_STRIDE + (CMP_BLOCK - 1)) <= t_col

    blk = lax.broadcasted_iota(jnp.int32, (SLC_SLOTS, tq), 0)
    t_row = t0 + lax.broadcasted_iota(jnp.int32, (SLC_SLOTS, tq), 1)
    cur = t_row // SLC_BLOCK
    forced = (blk == 0) | (blk == cur) | (blk == cur - 1)
    eligible = blk <= cur

    for g in range(NSA_KV_GROUPS):
        kc = kc_ref[g]
        vc = vc_ref[g]
        p_sum = jnp.zeros((tq, ncp), F32)
        for hh in range(NSA_HPG):
            hd = g * NSA_HPG + hh
            qh = q[:, hd * NSA_DH:(hd + 1) * NSA_DH] * (NSA_DH ** -0.5)
            s = jnp.where(visible, _dot_nt(qh, kc), NEG_BIG)
            e = jnp.where(visible, jnp.exp(s - jnp.max(s, axis=1, keepdims=True)), 0.0)
            p = e / jnp.maximum(jnp.sum(e, axis=1, keepdims=True), 1e-30)
            ocmp_ref[:, hd * NSA_DH:(hd + 1) * NSA_DH] = _dot(p.astype(BF16), vc).astype(ocmp_ref.dtype)
            p_sum = p_sum + p
        imp = jnp.dot(p_sum, map_ref[...], preferred_element_type=F32, precision=lax.Precision.HIGHEST)
        score = jnp.where(forced, SEL_BIG, jnp.where(eligible, imp.T, -SEL_BIG))
        sel = jnp.zeros(score.shape, jnp.bool_)
        for _ in range(SLC_TOPN):
            best = jnp.max(score, axis=0, keepdims=True)
            first = jnp.min(jnp.where(score == best, blk, SLC_SLOTS), axis=0, keepdims=True)
            hit = blk == first
            sel = sel | hit
            score = jnp.where(hit, -jnp.inf, score)
        neg = jnp.where(sel, 0.0, -MASK_BIG)
        mask_ref[:, g * SLC_SLOTS:(g + 1) * SLC_SLOTS] = neg.T.astype(mask_ref.dtype)


def _select(n_q, kc, vc, imp_map, tq):
    B, S, _ = n_q.shape
    G = NSA_KV_GROUPS
    ncp = kc.shape[2]
    return pl.pallas_call(
        _select_kernel,
        grid=(B, S // tq),
        in_specs=[
            pl.BlockSpec((None, tq, N_Q), lambda b, i: (b, i, 0)),
            pl.BlockSpec((None, G, ncp, NSA_DH), lambda b, i: (b, 0, 0, 0)),
            pl.BlockSpec((None, G, ncp, NSA_DH), lambda b, i: (b, 0, 0, 0)),
            pl.BlockSpec((ncp, SLC_SLOTS), lambda b, i: (0, 0)),
        ],
        out_specs=[
            pl.BlockSpec((None, tq, N_Q), lambda b, i: (b, i, 0)),
            pl.BlockSpec((None, tq, G * SLC_SLOTS), lambda b, i: (b, i, 0)),
        ],
        out_shape=[
            jax.ShapeDtypeStruct((B, S, N_Q), BF16),
            jax.ShapeDtypeStruct((B, S, G * SLC_SLOTS), BF16),
        ],
        compiler_params=_cparams(("parallel", "parallel")),
        name="nsa_select",
    )(n_q, kc, vc, imp_map)


def _importance_map(ncp):
    n_cmp = ncp - 1
    ratio, span = SLC_BLOCK // CMP_STRIDE, CMP_BLOCK // CMP_STRIDE
    offs = (np.arange(ratio)[:, None] - np.arange(span)[None, :]).reshape(-1)
    m = np.zeros((ncp, SLC_SLOTS), np.float32)
    n_slc = ncp * CMP_STRIDE // SLC_BLOCK
    for j in range(n_slc):
        for c in j * ratio + offs:
            if 0 <= c < n_cmp:
                m[c, j] += 1.0
    return jnp.asarray(m)


_TK = 256


def _attend_kernel(q_ref, mask_ref, ocmp_ref, gate_ref, kaug_ref, vaug_ref, kw_ref, vw_ref, y_ref,
                   qa_scr, acc_scr, m_scr):
    tq = q_ref.shape[0]
    S = kaug_ref.shape[1]
    rows = NSA_HPG * tq
    q0 = pl.program_id(1) * tq
    q = q_ref[...]
    gates = _sigmoid(gate_ref[...])
    t_rows = q0 + lax.broadcasted_iota(jnp.int32, (rows, 1), 0) % tq
    n_full = q0 // _TK
    wlen = tq + WINDOW
    w0 = pl.multiple_of(jnp.maximum(q0 - WINDOW, 0), tq)
    wlen = min(wlen, S)

    for g in range(NSA_KV_GROUPS):
        for hh in range(NSA_HPG):
            hd = g * NSA_HPG + hh
            qa_scr[hh * tq:(hh + 1) * tq, 0:SLC_SLOTS] = mask_ref[:, g * SLC_SLOTS:(g + 1) * SLC_SLOTS]
            qa_scr[hh * tq:(hh + 1) * tq, SLC_SLOTS:SLC_SLOTS + NSA_DH] = (
                q[:, hd * NSA_DH:(hd + 1) * NSA_DH] * (NSA_DH ** -0.5))
        qa = qa_scr[...]
        m_scr[...] = jnp.full(m_scr.shape, NEG_BIG, F32)
        acc_scr[...] = jnp.zeros_like(acc_scr)

        def sweep(j, diagonal):
            k0 = pl.multiple_of(j * _TK, _TK)
            s = _dot_nt(qa, kaug_ref[g, pl.ds(k0, _TK), :])
            if diagonal:
                kpos = k0 + lax.broadcasted_iota(jnp.int32, (1, _TK), 1)
                s = jnp.where(kpos <= t_rows, s, -MASK_BIG)
            m_old = m_scr[...]
            m_new = jnp.maximum(m_old, jnp.max(s, axis=1, keepdims=True))
            p = jnp.exp(s - m_new)
            acc_scr[...] = jnp.exp(m_old - m_new) * acc_scr[...] + _dot(
                p.astype(BF16), vaug_ref[g, pl.ds(k0, _TK), :])
            m_scr[...] = m_new

        def body(j, carry):
            sweep(j, False)
            return carry

        lax.fori_loop(0, n_full, body, 0)
        sweep(n_full, True)
        acc = acc_scr[...]
        o_slc = acc[:, :NSA_DH] / acc[:, NSA_DH:NSA_DH + 1]

        qs = qa[:, SLC_SLOTS:SLC_SLOTS + NSA_DH]
        s = _dot_nt(qs, kw_ref[g, pl.ds(w0, wlen), :])
        kpos = w0 + lax.broadcasted_iota(jnp.int32, (1, wlen), 1)
        band = (kpos <= t_rows) & (kpos > t_rows - WINDOW)
        s = jnp.where(band, s, NEG_BIG)
        e = jnp.where(band, jnp.exp(s - jnp.max(s, axis=1, keepdims=True)), 0.0)
        o_win = _dot(e.astype(BF16), vw_ref[g, pl.ds(w0, wlen), :]) / jnp.sum(e, axis=1, keepdims=True)

        for hh in range(NSA_HPG):
            hd = g * NSA_HPG + hh
            gc = GATE_COL0 + 3 * hd
            o = (gates[:, gc:gc + 1] * ocmp_ref[:, hd * NSA_DH:(hd + 1) * NSA_DH].astype(F32)
                 + gates[:, gc + 1:gc + 2] * o_slc[hh * tq:(hh + 1) * tq]
                 + gates[:, gc + 2:gc + 3] * o_win[hh * tq:(hh + 1) * tq])
            y_ref[:, hd * NSA_DH:(hd + 1) * NSA_DH] = o.astype(y_ref.dtype)


def _attend(n_q, mask, o_cmp, narrow, kaug, vaug, kw, vw, tq):
    B, S, _ = n_q.shape
    G = NSA_KV_GROUPS
    rows = NSA_HPG * tq
    full = lambda b, i: (b, 0, 0, 0)
    return pl.pallas_call(
        _attend_kernel,
        grid=(B, S // tq),
        in_specs=[
            pl.BlockSpec((None, tq, N_Q), lambda b, i: (b, i, 0)),
            pl.BlockSpec((None, tq, G * SLC_SLOTS), lambda b, i: (b, i, 0)),
            pl.BlockSpec((None, tq, N_Q), lambda b, i: (b, i, 0)),
            pl.BlockSpec((None, tq, _NARROW_W), lambda b, i: (b, i, 0)),
            pl.BlockSpec((None, G, S, SLC_SLOTS + NSA_DH), full),
            pl.BlockSpec((None, G, S, LANES), full),
            pl.BlockSpec((None, G, S, NSA_DH), full),
            pl.BlockSpec((None, G, S, NSA_DH), full),
        ],
        out_specs=pl.BlockSpec((None, tq, N_Q), lambda b, i: (b, i, 0)),
        out_shape=jax.ShapeDtypeStruct((B, S, N_Q), BF16),
        scratch_shapes=[
            pltpu.VMEM((rows, SLC_SLOTS + NSA_DH), BF16),
            pltpu.VMEM((rows, LANES), F32),
            pltpu.VMEM((rows, 1), F32),
        ],
        compiler_params=_cparams(("parallel", "arbitrary")),
        name="nsa_attend",
    )(n_q, mask, o_cmp, narrow, kaug, vaug, kw, vw)


def _merge_kernel(x_ref, ya_ref, yb_ref, ga_ref, gb_ref, wa_ref, wb_ref, wo_ref, g2_ref, x1_ref, h2_ref):
    a = _sigmoid(ga_ref[...].astype(F32)) * _dot(ya_ref[...], wa_ref[...])
    b = _sigmoid(gb_ref[...].astype(F32)) * _dot(yb_ref[...], wb_ref[...])
    x1 = x_ref[...] + _dot((a + b).astype(BF16), wo_ref[...])
    x1_ref[...] = x1
    h2_ref[...] = _rms(x1, g2_ref[...]).astype(h2_ref.dtype)


def _merge(x2d, ya, yb, ga, gb, wa, wb, wo, g2, tm):
    T = x2d.shape[0]
    row = lambda w: pl.BlockSpec((tm, w), lambda i: (i, 0))
    const = lambda r, c: pl.BlockSpec((r, c), lambda i: (0, 0))
    return pl.pallas_call(
        _merge_kernel,
        grid=(T // tm,),
        in_specs=[row(D_MODEL), row(M_V), row(N_Q), row(D_MODEL), row(D_MODEL),
                  const(M_V, D_MODEL), const(N_Q, D_MODEL), const(D_MODEL, D_MODEL), const(1, D_MODEL)],
        out_specs=[row(D_MODEL), row(D_MODEL)],
        out_shape=[jax.ShapeDtypeStruct((T, D_MODEL), F32), jax.ShapeDtypeStruct((T, D_MODEL), BF16)],
        compiler_params=_cparams(("parallel",)),
        name="merge",
    )(x2d, ya, yb, ga, gb, wa.astype(BF16), wb.astype(BF16), wo.astype(BF16), g2[None, :])


_FFN_CHUNK = 256


def _ffn_kernel(x1_ref, h2_ref, wg_ref, wu_ref, wd_ref, gf_ref, out_ref):
    h = h2_ref[...]
    acc = x1_ref[...]
    for c in range(FFN_HIDDEN // _FFN_CHUNK):
        sl = slice(c * _FFN_CHUNK, (c + 1) * _FFN_CHUNK)
        gate = _dot(h, wg_ref[:, sl])
        up = _dot(h, wu_ref[:, sl])
        acc = acc + _dot((gate * _sigmoid(gate) * up).astype(BF16), wd_ref[sl, :])
    out_ref[...] = _rms(acc, gf_ref[...])


def _ffn(x1, h2, wg, wu, wd, gf, tm):
    T = x1.shape[0]
    row = lambda: pl.BlockSpec((tm, D_MODEL), lambda i: (i, 0))
    const = lambda r, c: pl.BlockSpec((r, c), lambda i: (0, 0))
    return pl.pallas_call(
        _ffn_kernel,
        grid=(T // tm,),
        in_specs=[row(), row(), const(D_MODEL, FFN_HIDDEN), const(D_MODEL, FFN_HIDDEN),
                  const(FFN_HIDDEN, D_MODEL), const(1, D_MODEL)],
        out_specs=row(),
        out_shape=jax.ShapeDtypeStruct((T, D_MODEL), F32),
        compiler_params=_cparams(("parallel",)),
        name="ffn",
    )(x1, h2, wg.astype(BF16), wu.astype(BF16), wd.astype(BF16), gf[None, :])


def _layer(x, norm1_g, w_in, b_in, f_bias, conv_w, conv_b, mlstm_norm_g, cmp_k_pos, cmp_k_w1, cmp_k_w2,
           cmp_v_pos, cmp_v_w1, cmp_v_w2, w_branch_a, w_branch_b, w_out, norm2_g):
    B, S, D = x.shape
    T = B * S
    G, DH, H = NSA_KV_GROUPS, NSA_DH, MLSTM_HEADS
    x2d = x.reshape(T, D)
    tm = min(256, T)

    m_qk, m_v, m_o, n_q, n_kv, gate_a, gate_b, narrow = _inproj(x2d, norm1_g, w_in, b_in, tm)

    q, k = _conv(m_qk.reshape(B, S, 2 * M_QK), conv_w, conv_b, min(512, S))
    gates = narrow[:, :2 * H].reshape(B, S, 2 * H)
    gates_row = gates.transpose(0, 2, 1)
    gates_col = gates.reshape(B, S, 2, H).transpose(0, 3, 1, 2)
    y_a = _mlstm(q, k, m_v.reshape(B, S, M_V), m_o.reshape(B, S, M_V), gates_row, gates_col,
                 f_bias, mlstm_norm_g, min(256, S))

    kv = n_kv.reshape(B, S, 6, G, DH).transpose(2, 0, 3, 1, 4)
    ncp = S // CMP_STRIDE
    strips = kv[0:2].reshape(2, B, G, ncp, CMP_STRIDE * DH)
    pos = jnp.stack([cmp_k_pos, cmp_v_pos]).reshape(2, 1, CMP_BLOCK * DH)
    cmp = _compress(strips, pos, jnp.stack([cmp_k_w1, cmp_v_w1]), jnp.stack([cmp_k_w2, cmp_v_w2]))
    n_q3 = n_q.reshape(B, S, N_Q)
    tq = min(128, S)
    o_cmp, mask = _select(n_q3, cmp[0], cmp[1], _importance_map(ncp), tq)
    onehot = jax.nn.one_hot(jnp.arange(S) // SLC_BLOCK, SLC_SLOTS, dtype=BF16)
    kaug = jnp.concatenate([jnp.broadcast_to(onehot, (B, G, S, SLC_SLOTS)), kv[2]], axis=-1)
    ones = jnp.zeros((S, LANES - DH), BF16).at[:, 0].set(1.0)
    vaug = jnp.concatenate([kv[3], jnp.broadcast_to(ones, (B, G, S, LANES - DH))], axis=-1)
    y_b = _attend(n_q3, mask, o_cmp, narrow.reshape(B, S, _NARROW_W), kaug, vaug, kv[4], kv[5], tq)

    return _merge(x2d, y_a.reshape(T, M_V), y_b.reshape(T, N_Q), gate_a, gate_b,
                  w_branch_a, w_branch_b, w_out, norm2_g, tm)


def kernel(x, norm1_g, w_in, b_in, f_bias, conv_w, conv_b, mlstm_norm_g, cmp_k_pos, cmp_k_w1, cmp_k_w2,
           cmp_v_pos, cmp_v_w1, cmp_v_w2, w_branch_a, w_branch_b, w_out, norm2_g, w_ffn_gate, w_ffn_up,
           w_ffn_down, norm_f_g):
    B, S, D = x.shape
    depth = w_in.shape[0]
    assert depth == 1, "the fused final norm assumes a single layer"
    x1, h2 = _layer(x, norm1_g[0], w_in[0], b_in[0], f_bias[0], conv_w[0], conv_b[0], mlstm_norm_g[0],
                    cmp_k_pos[0], cmp_k_w1[0], cmp_k_w2[0], cmp_v_pos[0], cmp_v_w1[0], cmp_v_w2[0],
                    w_branch_a[0], w_branch_b[0], w_out[0], norm2_g[0])
    out = _ffn(x1, h2, w_ffn_gate[0], w_ffn_up[0], w_ffn_down[0], norm_f_g, min(256, B * S))
    return out.reshape(B, S, D)
```

```python
import functools

import numpy as np
import jax
import jax.numpy as jnp
from jax import lax
from jax.experimental import pallas as pl
from jax.experimental.pallas import tpu as pltpu

F32 = jnp.float32
BF16 = jnp.bfloat16

D_MODEL = 1024
MLSTM_HEADS = 4
MLSTM_DV = D_MODEL // MLSTM_HEADS
MLSTM_DQK = MLSTM_DV // 2
CONV_K = 4
NSA_DH = 64
NSA_HEADS = (D_MODEL // 2) // NSA_DH
NSA_KV_GROUPS = 2
NSA_HPG = NSA_HEADS // NSA_KV_GROUPS
CMP_BLOCK = 32
CMP_STRIDE = 16
CMP_HIDDEN = 256
SLC_BLOCK = 64
SLC_TOPN = 16
WINDOW = 512
FFN_HIDDEN = 2816
RMS_EPS = 1e-6
SEL_BIG = 1e9

M_QK = MLSTM_HEADS * MLSTM_DQK
M_V = MLSTM_HEADS * MLSTM_DV
N_Q = NSA_HEADS * NSA_DH
N_KV = NSA_KV_GROUPS * NSA_DH
IN_SPLITS = (2 * M_QK, M_V, M_V, MLSTM_HEADS, MLSTM_HEADS, N_Q, 6 * N_KV, 3 * NSA_HEADS, D_MODEL, D_MODEL)

LANES = 128
SLC_SLOTS = 128
MASK_BIG = float(2.0 ** 100)
NEG_BIG = -1e30
VMEM_LIMIT = 56 * 1024 * 1024

_SEG = dict(zip(("m_qk", "m_v", "m_o", "m_i", "m_f", "n_q", "n_kv", "n_g", "gate_a", "gate_b"),
                zip(np.cumsum((0,) + IN_SPLITS[:-1]).tolist(), IN_SPLITS)))
_WIDE = ("m_qk", "m_v", "m_o", "n_q", "n_kv", "gate_a", "gate_b")
_NARROW = ("m_i", "m_f", "n_g")
_NARROW_W = LANES
GATE_COL0 = 2 * MLSTM_HEADS


def _cparams(sem):
    return pltpu.CompilerParams(dimension_semantics=sem, vmem_limit_bytes=VMEM_LIMIT)


def _sigmoid(x):
    return 1.0 / (1.0 + jnp.exp(-x))


def _log_sigmoid(x):
    return jnp.minimum(x, 0.0) - jnp.log(1.0 + jnp.exp(-jnp.abs(x)))


def _rms(x, g):
    ms = jnp.mean(x * x, axis=-1, keepdims=True)
    return x * lax.rsqrt(ms + RMS_EPS) * g


def _dot_nt(a, b):
    return lax.dot_general(a, b, (((1,), (1,)), ((), ())), preferred_element_type=F32)


def _dot(a, b):
    return jnp.dot(a, b, preferred_element_type=F32)


def _inproj_kernel(x_ref, g_ref, w_ref, b_ref, *out_refs, widths):
    h = _rms(x_ref[...], g_ref[...]).astype(BF16)
    off = 0
    for o_ref, width in zip(out_refs, widths):
        acc = _dot(h, w_ref[:, off:off + width]) + b_ref[:, off:off + width]
        o_ref[...] = acc.astype(o_ref.dtype)
        off += width


def _inproj(x2d, norm_g, w_in, b_in, tm):
    T = x2d.shape[0]
    cols = [w_in[:, _SEG[n][0]:_SEG[n][0] + _SEG[n][1]] for n in _WIDE]
    bias = [b_in[_SEG[n][0]:_SEG[n][0] + _SEG[n][1]] for n in _WIDE]
    narrow_w = jnp.concatenate([w_in[:, _SEG[n][0]:_SEG[n][0] + _SEG[n][1]] for n in _NARROW], axis=1)
    narrow_b = jnp.concatenate([b_in[_SEG[n][0]:_SEG[n][0] + _SEG[n][1]] for n in _NARROW])
    pad = _NARROW_W - narrow_w.shape[1]
    cols.append(jnp.pad(narrow_w, ((0, 0), (0, pad))))
    bias.append(jnp.pad(narrow_b, (0, pad)))
    w = jnp.concatenate(cols, axis=1).astype(BF16)
    b = jnp.concatenate(bias)[None, :].astype(F32)
    widths = tuple(_SEG[n][1] for n in _WIDE) + (_NARROW_W,)
    dtypes = (BF16,) * len(_WIDE) + (F32,)
    n_tot = w.shape[1]
    return pl.pallas_call(
        functools.partial(_inproj_kernel, widths=widths),
        grid=(T // tm,),
        in_specs=[
            pl.BlockSpec((tm, D_MODEL), lambda i: (i, 0)),
            pl.BlockSpec((1, D_MODEL), lambda i: (0, 0)),
            pl.BlockSpec((D_MODEL, n_tot), lambda i: (0, 0)),
            pl.BlockSpec((1, n_tot), lambda i: (0, 0)),
        ],
        out_specs=[pl.BlockSpec((tm, wd), lambda i: (i, 0)) for wd in widths],
        out_shape=[jax.ShapeDtypeStruct((T, wd), dt) for wd, dt in zip(widths, dtypes)],
        compiler_params=_cparams(("parallel",)),
        name="inproj",
    )(x2d, norm_g[None, :], w, b)


_HALO = 16


def _conv_kernel(x_ref, halo_ref, w_ref, b_ref, q_ref, k_ref):
    i = pl.program_id(1)
    ts = x_ref.shape[0]
    x = x_ref[...].astype(F32)
    halo = jnp.where(i > 0, halo_ref[...].astype(F32), 0.0)
    xx = jnp.concatenate([halo, x], axis=0)
    acc = x * w_ref[CONV_K - 1:CONV_K, :] + b_ref[...]
    for j in range(1, CONV_K):
        shifted = pltpu.roll(xx, j, 0)[_HALO:_HALO + ts]
        acc = acc + shifted * w_ref[CONV_K - 1 - j:CONV_K - j, :]
    y = acc * _sigmoid(acc)
    q_ref[...] = y[:, :M_QK].astype(BF16)
    k_ref[...] = (y[:, M_QK:] * (MLSTM_DQK ** -0.5)).astype(BF16)


def _conv(m_qk, conv_w, conv_b, ts):
    B, S, C = m_qk.shape
    per = ts // _HALO
    return pl.pallas_call(
        _conv_kernel,
        grid=(B, S // ts),
        in_specs=[
            pl.BlockSpec((None, ts, C), lambda b, i: (b, i, 0)),
            pl.BlockSpec((None, _HALO, C), lambda b, i: (b, jnp.maximum(i * per - 1, 0), 0)),
            pl.BlockSpec((CONV_K, C), lambda b, i: (0, 0)),
            pl.BlockSpec((1, C), lambda b, i: (0, 0)),
        ],
        out_specs=[pl.BlockSpec((None, ts, M_QK), lambda b, i: (b, i, 0))] * 2,
        out_shape=[jax.ShapeDtypeStruct((B, S, M_QK), BF16)] * 2,
        compiler_params=_cparams(("parallel", "parallel")),
        name="mlstm_conv",
    )(m_qk, m_qk, conv_w, conv_b[None, :])


def _mlstm_kernel(q_ref, k_ref, v_ref, o_ref, gr_ref, gc_ref, fb_ref, ng_ref, y_ref, c_scr, n_scr, m_scr):
    hd = pl.program_id(1)
    L = q_ref.shape[0]

    @pl.when(pl.program_id(2) == 0)
    def _():
        c_scr[...] = jnp.zeros_like(c_scr)
        n_scr[...] = jnp.zeros_like(n_scr)
        m_scr[...] = jnp.zeros_like(m_scr)

    q = q_ref[...]
    k = k_ref[...]
    v = v_ref[...]
    fb = fb_ref[:, 0:1]
    i_row = gr_ref[pl.ds(hd, 1), :]
    lf_row = _log_sigmoid(gr_ref[pl.ds(MLSTM_HEADS + hd, 1), :] + fb)
    i_col = gc_ref[:, 0:1]
    lf_col = _log_sigmoid(gc_ref[:, 1:2] + fb)
    m_prev = m_scr[...]

    r_t = lax.broadcasted_iota(jnp.int32, (L, L), 0)
    r_s = lax.broadcasted_iota(jnp.int32, (L, L), 1)
    causal = r_s <= r_t
    b_col = jnp.sum(jnp.where(causal, lf_row, 0.0), axis=1, keepdims=True)
    b_row = jnp.sum(jnp.where(r_t <= r_s, lf_col, 0.0), axis=0, keepdims=True)
    b_last = jnp.sum(lf_row, axis=1, keepdims=True)

    d_intra = jnp.where(causal, b_col - b_row + i_row, NEG_BIG)
    d_inter = b_col + m_prev
    m_t = jnp.maximum(d_inter, jnp.max(d_intra, axis=1, keepdims=True))
    w_intra = jnp.exp(d_intra - m_t)
    w_inter = jnp.exp(d_inter - m_t)
    s = _dot_nt(q, k) * w_intra
    c_state = c_scr[...]
    n_state = n_scr[...]
    num = _dot(s.astype(BF16), v) + w_inter * _dot_nt(q, c_state.astype(BF16))
    nq = jnp.sum(s, axis=1, keepdims=True) + w_inter * jnp.sum(q.astype(F32) * n_state, axis=1, keepdims=True)
    h = num / jnp.maximum(jnp.abs(nq), jnp.exp(-m_t))
    h = _rms(h, ng_ref[...])
    y_ref[...] = (_sigmoid(o_ref[...].astype(F32)) * h).astype(y_ref.dtype)

    d_state_col = b_last - b_col + i_col
    d_state_row = b_last - b_row + i_row
    m_new = jnp.maximum(b_last + m_prev, jnp.max(d_state_row, axis=1, keepdims=True))
    w_state = jnp.exp(d_state_col - m_new)
    decay = jnp.exp(b_last + m_prev - m_new)
    vw_t = (v.astype(F32) * w_state).T.astype(BF16)
    c_scr[...] = decay * c_state + _dot(vw_t, k)
    n_scr[...] = decay * n_state + jnp.sum(w_state * k.astype(F32), axis=0, keepdims=True)
    m_scr[...] = m_new


def _mlstm(q, k, v, o, gates_row, gates_col, f_bias, norm_g, L):
    B, S, _ = q.shape
    H, DK, DV = MLSTM_HEADS, MLSTM_DQK, MLSTM_DV
    fb = jnp.broadcast_to(f_bias.astype(F32)[:, None, None], (H, 1, LANES))
    return pl.pallas_call(
        _mlstm_kernel,
        grid=(B, H, S // L),
        in_specs=[
            pl.BlockSpec((None, L, DK), lambda b, h, c: (b, c, h)),
            pl.BlockSpec((None, L, DK), lambda b, h, c: (b, c, h)),
            pl.BlockSpec((None, L, DV), lambda b, h, c: (b, c, h)),
            pl.BlockSpec((None, L, DV), lambda b, h, c: (b, c, h)),
            pl.BlockSpec((None, 2 * H, L), lambda b, h, c: (b, 0, c)),
            pl.BlockSpec((None, None, L, 2), lambda b, h, c: (b, h, c, 0)),
            pl.BlockSpec((None, 1, LANES), lambda b, h, c: (h, 0, 0)),
            pl.BlockSpec((1, DV), lambda b, h, c: (0, h)),
        ],
        out_specs=pl.BlockSpec((None, L, DV), lambda b, h, c: (b, c, h)),
        out_shape=jax.ShapeDtypeStruct((B, S, M_V), BF16),
        scratch_shapes=[pltpu.VMEM((DV, DK), F32), pltpu.VMEM((1, DK), F32), pltpu.VMEM((1, 1), F32)],
        compiler_params=_cparams(("parallel", "parallel", "arbitrary")),
        name="mlstm",
    )(q, k, v, o, gates_row, gates_col, fb, norm_g[None, :])


def _compress_kernel(s_ref, pos_ref, w1_ref, w2_ref, out_ref):
    n = s_ref.shape[0]
    half = w1_ref.shape[0] // 2
    strips = s_ref[...]
    w1 = w1_ref[...].astype(BF16)
    first = _dot(strips, w1[:half])
    second = _dot(strips, w1[half:])
    pos_term = _dot(jnp.broadcast_to(pos_ref[...], (8, 2 * half)).astype(BF16), w1)[0:1]
    hid = first + pltpu.roll(second, n - 1, 0) + pos_term
    hid = hid * _sigmoid(hid)
    out = _dot(hid.astype(BF16), w2_ref[...].astype(BF16))
    row = lax.broadcasted_iota(jnp.int32, out.shape, 0)
    out_ref[...] = jnp.where(row < n - 1, out, 0.0).astype(out_ref.dtype)


def _compress(strips, pos, w1, w2):
    _, B, G, n, sw = strips.shape
    return pl.pallas_call(
        _compress_kernel,
        grid=(2, B, G),
        in_specs=[
            pl.BlockSpec((None, None, None, n, sw), lambda a, b, g: (a, b, g, 0, 0)),
            pl.BlockSpec((None, 1, 2 * sw), lambda a, b, g: (a, 0, 0)),
            pl.BlockSpec((None, 2 * sw, CMP_HIDDEN), lambda a, b, g: (a, 0, 0)),
            pl.BlockSpec((None, CMP_HIDDEN, NSA_DH), lambda a, b, g: (a, 0, 0)),
        ],
        out_specs=pl.BlockSpec((None, None, None, n, NSA_DH), lambda a, b, g: (a, b, g, 0, 0)),
        out_shape=jax.ShapeDtypeStruct((2, B, G, n, NSA_DH), BF16),
        compiler_params=_cparams(("parallel", "parallel", "parallel")),
        name="nsa_compress",
    )(strips, pos, w1, w2)


def _select_kernel(q_ref, kc_ref, vc_ref, map_ref, ocmp_ref, mask_ref):
    tq = q_ref.shape[0]
    ncp = kc_ref.shape[1]
    t0 = pl.program_id(1) * tq
    q = q_ref[...]
    t_col = t0 + lax.broadcasted_iota(jnp.int32, (tq, 1), 0)
    c_row = lax.broadcasted_iota(jnp.int32, (1, ncp), 1)
    visible = (c_row * CMP_STRIDE + (CMP_BLOCK - 1)) <= t_col

    blk = lax.broadcasted_iota(jnp.int32, (SLC_SLOTS, tq), 0)
    t_row = t0 + lax.broadcasted_iota(jnp.int32, (SLC_SLOTS, tq), 1)
    cur = t_row // SLC_BLOCK
    forced = (blk == 0) | (blk == cur) | (blk == cur - 1)
    eligible = blk <= cur

    for g in range(NSA_KV_GROUPS):
        kc = kc_ref[g]
        vc = vc_ref[g]
        p_sum = jnp.zeros((tq, ncp), F32)
        for hh in range(NSA_HPG):
            hd = g * NSA_HPG + hh
            qh = q[:, hd * NSA_DH:(hd + 1) * NSA_DH] * (NSA_DH ** -0.5)
            s = jnp.where(visible, _dot_nt(qh, kc), NEG_BIG)
            e = jnp.where(visible, jnp.exp(s - jnp.max(s, axis=1, keepdims=True)), 0.0)
            p = e / jnp.maximum(jnp.sum(e, axis=1, keepdims=True), 1e-30)
            ocmp_ref[:, hd * NSA_DH:(hd + 1) * NSA_DH] = _dot(p.astype(BF16), vc).astype(ocmp_ref.dtype)
            p_sum = p_sum + p
        imp = jnp.dot(p_sum, map_ref[...], preferred_element_type=F32, precision=lax.Precision.HIGHEST)
        score = jnp.where(forced, SEL_BIG, jnp.where(eligible, imp.T, -SEL_BIG))
        sel = jnp.zeros(score.shape, jnp.bool_)
        for _ in range(SLC_TOPN):
            best = jnp.max(score, axis=0, keepdims=True)
            first = jnp.min(jnp.where(score == best, blk, SLC_SLOTS), axis=0, keepdims=True)
            hit = blk == first
            sel = sel | hit
            score = jnp.where(hit, -jnp.inf, score)
        neg = jnp.where(sel, 0.0, -MASK_BIG)
        mask_ref[:, g * SLC_SLOTS:(g + 1) * SLC_SLOTS] = neg.T.astype(mask_ref.dtype)


def _select(n_q, kc, vc, imp_map, tq):
    B, S, _ = n_q.shape
    G = NSA_KV_GROUPS
    ncp = kc.shape[2]
    return pl.pallas_call(
        _select_kernel,
        grid=(B, S // tq),
        in_specs=[
            pl.BlockSpec((None, tq, N_Q), lambda b, i: (b, i, 0)),
            pl.BlockSpec((None, G, ncp, NSA_DH), lambda b, i: (b, 0, 0, 0)),
            pl.BlockSpec((None, G, ncp, NSA_DH), lambda b, i: (b, 0, 0, 0)),
            pl.BlockSpec((ncp, SLC_SLOTS), lambda b, i: (0, 0)),
        ],
        out_specs=[
            pl.BlockSpec((None, tq, N_Q), lambda b, i: (b, i, 0)),
            pl.BlockSpec((None, tq, G * SLC_SLOTS), lambda b, i: (b, i, 0)),
        ],
        out_shape=[
            jax.ShapeDtypeStruct((B, S, N_Q), BF16),
            jax.ShapeDtypeStruct((B, S, G * SLC_SLOTS), BF16),
        ],
        compiler_params=_cparams(("parallel", "parallel")),
        name="nsa_select",
    )(n_q, kc, vc, imp_map)


def _importance_map(ncp):
    n_cmp = ncp - 1
    ratio, span = SLC_BLOCK // CMP_STRIDE, CMP_BLOCK // CMP_STRIDE
    offs = (np.arange(ratio)[:, None] - np.arange(span)[None, :]).reshape(-1)
    m = np.zeros((ncp, SLC_SLOTS), np.float32)
    n_slc = ncp * CMP_STRIDE // SLC_BLOCK
    for j in range(n_slc):
        for c in j * ratio + offs:
            if 0 <= c < n_cmp:
                m[c, j] += 1.0
    return jnp.asarray(m)


_TK = 256
_KAUG_W = 2 * LANES
_BOUND_COL = SLC_SLOTS + NSA_DH


def _attend_kernel(q_ref, mask_ref, ocmp_ref, gate_ref, kaug_ref, vaug_ref, kw_ref, vw_ref, y_ref,
                   qa_scr, mx_scr, acc_scr):
    tq = q_ref.shape[0]
    S = kaug_ref.shape[1]
    G = NSA_KV_GROUPS
    rows = NSA_HPG * tq
    q0 = pl.program_id(1) * tq
    q = q_ref[...]
    t_rows = q0 + lax.broadcasted_iota(jnp.int32, (rows, 1), 0) % tq
    n_full = q0 // _TK
    kd = pl.multiple_of(n_full * _TK, _TK)
    causal = (kd + lax.broadcasted_iota(jnp.int32, (1, _TK), 1)) <= t_rows

    qa_scr[...] = jnp.zeros_like(qa_scr)
    for g in range(G):
        for hh in range(NSA_HPG):
            hd = g * NSA_HPG + hh
            qa_scr[g, hh * tq:(hh + 1) * tq, 0:SLC_SLOTS] = mask_ref[:, g * SLC_SLOTS:(g + 1) * SLC_SLOTS]
            qa_scr[g, hh * tq:(hh + 1) * tq, SLC_SLOTS:SLC_SLOTS + NSA_DH] = (
                q[:, hd * NSA_DH:(hd + 1) * NSA_DH] * (NSA_DH ** -0.5))

    def scores(g, k0):
        return _dot_nt(qa_scr[g], kaug_ref[g, pl.ds(k0, _TK), :])

    mx_scr[...] = jnp.full(mx_scr.shape, NEG_BIG, F32)

    def max_step(k0, diagonal):
        for g in range(G):
            s = scores(g, k0)
            if diagonal:
                s = jnp.where(causal, s, -MASK_BIG)
            mx_scr[g] = jnp.maximum(mx_scr[g], jnp.maximum(s[:, :LANES], s[:, LANES:]))

    def max_body(j, carry):
        max_step(pl.multiple_of(j * _TK, _TK), False)
        return carry

    lax.fori_loop(0, n_full, max_body, 0)
    max_step(kd, True)
    for g in range(G):
        bound = jnp.max(mx_scr[g], axis=1, keepdims=True)
        qa_scr[g, :, _BOUND_COL:_BOUND_COL + 1] = (-bound).astype(BF16)

    acc_scr[...] = jnp.zeros_like(acc_scr)

    def sum_step(k0, diagonal):
        for g in range(G):
            s = scores(g, k0)
            if diagonal:
                s = jnp.where(causal, s, -MASK_BIG)
            acc_scr[g] += _dot(jnp.exp(s).astype(BF16), vaug_ref[g, pl.ds(k0, _TK), :])

    def sum_body(j, carry):
        sum_step(pl.multiple_of(j * _TK, _TK), False)
        return carry

    lax.fori_loop(0, n_full, sum_body, 0)
    sum_step(kd, True)

    gates = _sigmoid(gate_ref[...])
    wlen = min(tq + WINDOW, S)
    w0 = pl.multiple_of(jnp.maximum(q0 - WINDOW, 0), tq)
    kpos = w0 + lax.broadcasted_iota(jnp.int32, (1, wlen), 1)
    band = (kpos <= t_rows) & (kpos > t_rows - WINDOW)
    for g in range(G):
        acc = acc_scr[g]
        o_slc = acc[:, :NSA_DH] / acc[:, NSA_DH:NSA_DH + 1]
        qs = qa_scr[g, :, SLC_SLOTS:SLC_SLOTS + NSA_DH]
        s = jnp.where(band, _dot_nt(qs, kw_ref[g, pl.ds(w0, wlen), :]), NEG_BIG)
        e = jnp.exp(s - jnp.max(s, axis=1, keepdims=True))
        o_win = _dot(e.astype(BF16), vw_ref[g, pl.ds(w0, wlen), :]) / jnp.sum(e, axis=1, keepdims=True)
        for hh in range(NSA_HPG):
            hd = g * NSA_HPG + hh
            gc = GATE_COL0 + 3 * hd
            o = (gates[:, gc:gc + 1] * ocmp_ref[:, hd * NSA_DH:(hd + 1) * NSA_DH].astype(F32)
                 + gates[:, gc + 1:gc + 2] * o_slc[hh * tq:(hh + 1) * tq]
                 + gates[:, gc + 2:gc + 3] * o_win[hh * tq:(hh + 1) * tq])
            y_ref[:, hd * NSA_DH:(hd + 1) * NSA_DH] = o.astype(y_ref.dtype)


def _attend(n_q, mask, o_cmp, narrow, kaug, vaug, kw, vw, tq):
    B, S, _ = n_q.shape
    G = NSA_KV_GROUPS
    rows = NSA_HPG * tq
    full = lambda b, i: (b, 0, 0, 0)
    return pl.pallas_call(
        _attend_kernel,
        grid=(B, S // tq),
        in_specs=[
            pl.BlockSpec((None, tq, N_Q), lambda b, i: (b, i, 0)),
            pl.BlockSpec((None, tq, G * SLC_SLOTS), lambda b, i: (b, i, 0)),
            pl.BlockSpec((None, tq, N_Q), lambda b, i: (b, i, 0)),
            pl.BlockSpec((None, tq, _NARROW_W), lambda b, i: (b, i, 0)),
            pl.BlockSpec((None, G, S, _KAUG_W), full),
            pl.BlockSpec((None, G, S, LANES), full),
            pl.BlockSpec((None, G, S, NSA_DH), full),
            pl.BlockSpec((None, G, S, NSA_DH), full),
        ],
        out_specs=pl.BlockSpec((None, tq, N_Q), lambda b, i: (b, i, 0)),
        out_shape=jax.ShapeDtypeStruct((B, S, N_Q), BF16),
        scratch_shapes=[
            pltpu.VMEM((G, rows, _KAUG_W), BF16),
            pltpu.VMEM((G, rows, LANES), F32),
            pltpu.VMEM((G, rows, LANES), F32),
        ],
        compiler_params=_cparams(("parallel", "arbitrary")),
        name="nsa_attend",
    )(n_q, mask, o_cmp, narrow, kaug, vaug, kw, vw)


def _merge_kernel(x_ref, ya_ref, yb_ref, ga_ref, gb_ref, wa_ref, wb_ref, wo_ref, g2_ref, x1_ref, h2_ref):
    a = _sigmoid(ga_ref[...].astype(F32)) * _dot(ya_ref[...], wa_ref[...])
    b = _sigmoid(gb_ref[...].astype(F32)) * _dot(yb_ref[...], wb_ref[...])
    x1 = x_ref[...] + _dot((a + b).astype(BF16), wo_ref[...])
    x1_ref[...] = x1
    h2_ref[...] = _rms(x1, g2_ref[...]).astype(h2_ref.dtype)


def _merge(x2d, ya, yb, ga, gb, wa, wb, wo, g2, tm):
    T = x2d.shape[0]
    row = lambda w: pl.BlockSpec((tm, w), lambda i: (i, 0))
    const = lambda r, c: pl.BlockSpec((r, c), lambda i: (0, 0))
    return pl.pallas_call(
        _merge_kernel,
        grid=(T // tm,),
        in_specs=[row(D_MODEL), row(M_V), row(N_Q), row(D_MODEL), row(D_MODEL),
                  const(M_V, D_MODEL), const(N_Q, D_MODEL), const(D_MODEL, D_MODEL), const(1, D_MODEL)],
        out_specs=[row(D_MODEL), row(D_MODEL)],
        out_shape=[jax.ShapeDtypeStruct((T, D_MODEL), F32), jax.ShapeDtypeStruct((T, D_MODEL), BF16)],
        compiler_params=_cparams(("parallel",)),
        name="merge",
    )(x2d, ya, yb, ga, gb, wa.astype(BF16), wb.astype(BF16), wo.astype(BF16), g2[None, :])


_FFN_CHUNK = 256


def _ffn_kernel(x1_ref, h2_ref, wg_ref, wu_ref, wd_ref, gf_ref, out_ref):
    h = h2_ref[...]
    acc = x1_ref[...]
    for c in range(FFN_HIDDEN // _FFN_CHUNK):
        sl = slice(c * _FFN_CHUNK, (c + 1) * _FFN_CHUNK)
        gate = _dot(h, wg_ref[:, sl])
        up = _dot(h, wu_ref[:, sl])
        acc = acc + _dot((gate * _sigmoid(gate) * up).astype(BF16), wd_ref[sl, :])
    out_ref[...] = _rms(acc, gf_ref[...])


def _ffn(x1, h2, wg, wu, wd, gf, tm):
    T = x1.shape[0]
    row = lambda: pl.BlockSpec((tm, D_MODEL), lambda i: (i, 0))
    const = lambda r, c: pl.BlockSpec((r, c), lambda i: (0, 0))
    return pl.pallas_call(
        _ffn_kernel,
        grid=(T // tm,),
        in_specs=[row(), row(), const(D_MODEL, FFN_HIDDEN), const(D_MODEL, FFN_HIDDEN),
                  const(FFN_HIDDEN, D_MODEL), const(1, D_MODEL)],
        out_specs=row(),
        out_shape=jax.ShapeDtypeStruct((T, D_MODEL), F32),
        compiler_params=_cparams(("parallel",)),
        name="ffn",
    )(x1, h2, wg.astype(BF16), wu.astype(BF16), wd.astype(BF16), gf[None, :])


def _layer(x, norm1_g, w_in, b_in, f_bias, conv_w, conv_b, mlstm_norm_g, cmp_k_pos, cmp_k_w1, cmp_k_w2,
           cmp_v_pos, cmp_v_w1, cmp_v_w2, w_branch_a, w_branch_b, w_out, norm2_g):
    B, S, D = x.shape
    T = B * S
    G, DH, H = NSA_KV_GROUPS, NSA_DH, MLSTM_HEADS
    x2d = x.reshape(T, D)
    tm = min(256, T)

    m_qk, m_v, m_o, n_q, n_kv, gate_a, gate_b, narrow = _inproj(x2d, norm1_g, w_in, b_in, tm)

    q, k = _conv(m_qk.reshape(B, S, 2 * M_QK), conv_w, conv_b, min(512, S))
    gates = narrow[:, :2 * H].reshape(B, S, 2 * H)
    gates_row = gates.transpose(0, 2, 1)
    gates_col = gates.reshape(B, S, 2, H).transpose(0, 3, 1, 2)
    y_a = _mlstm(q, k, m_v.reshape(B, S, M_V), m_o.reshape(B, S, M_V), gates_row, gates_col,
                 f_bias, mlstm_norm_g, min(256, S))

    kv = n_kv.reshape(B, S, 6, G, DH).transpose(2, 0, 3, 1, 4)
    ncp = S // CMP_STRIDE
    strips = kv[0:2].reshape(2, B, G, ncp, CMP_STRIDE * DH)
    pos = jnp.stack([cmp_k_pos, cmp_v_pos]).reshape(2, 1, CMP_BLOCK * DH)
    cmp = _compress(strips, pos, jnp.stack([cmp_k_w1, cmp_v_w1]), jnp.stack([cmp_k_w2, cmp_v_w2]))
    n_q3 = n_q.reshape(B, S, N_Q)
    tq = min(128, S)
    o_cmp, mask = _select(n_q3, cmp[0], cmp[1], _importance_map(ncp), tq)
    onehot = jax.nn.one_hot(jnp.arange(S) // SLC_BLOCK, SLC_SLOTS, dtype=BF16)
    ones = jnp.zeros((S, LANES - DH), BF16).at[:, 0].set(1.0)
    kaug = jnp.concatenate([jnp.broadcast_to(onehot, (B, G, S, SLC_SLOTS)), kv[2],
                            jnp.broadcast_to(ones, (B, G, S, LANES - DH))], axis=-1)
    vaug = jnp.concatenate([kv[3], jnp.broadcast_to(ones, (B, G, S, LANES - DH))], axis=-1)
    y_b = _attend(n_q3, mask, o_cmp, narrow.reshape(B, S, _NARROW_W), kaug, vaug, kv[4], kv[5], min(256, S))

    return _merge(x2d, y_a.reshape(T, M_V), y_b.reshape(T, N_Q), gate_a, gate_b,
                  w_branch_a, w_branch_b, w_out, norm2_g, tm)


def kernel(x, norm1_g, w_in, b_in, f_bias, conv_w, conv_b, mlstm_norm_g, cmp_k_pos, cmp_k_w1, cmp_k_w2,
           cmp_v_pos, cmp_v_w1, cmp_v_w2, w_branch_a, w_branch_b, w_out, norm2_g, w_ffn_gate, w_ffn_up,
           w_ffn_down, norm_f_g):
    B, S, D = x.shape
    depth = w_in.shape[0]
    assert depth == 1, "the fused final norm assumes a single layer"
    x1, h2 = _layer(x, norm1_g[0], w_in[0], b_in[0], f_bias[0], conv_w[0], conv_b[0], mlstm_norm_g[0],
                    cmp_k_pos[0], cmp_k_w1[0], cmp_k_w2[0], cmp_v_pos[0], cmp_v_w1[0], cmp_v_w2[0],
                    w_branch_a[0], w_branch_b[0], w_out[0], norm2_g[0])
    out = _ffn(x1, h2, w_ffn_gate[0], w_ffn_up[0], w_ffn_down[0], norm_f_g, min(256, B * S))
    return out.reshape(B, S, D)
```

```python
import functools

import numpy as np
import jax
import jax.numpy as jnp
from jax import lax
from jax.experimental import pallas as pl
from jax.experimental.pallas import tpu as pltpu

F32 = jnp.float32
BF16 = jnp.bfloat16

D_MODEL = 1024
MLSTM_HEADS = 4
MLSTM_DV = D_MODEL // MLSTM_HEADS
MLSTM_DQK = MLSTM_DV // 2
CONV_K = 4
NSA_DH = 64
NSA_HEADS = (D_MODEL // 2) // NSA_DH
NSA_KV_GROUPS = 2
NSA_HPG = NSA_HEADS // NSA_KV_GROUPS
CMP_BLOCK = 32
CMP_STRIDE = 16
CMP_HIDDEN = 256
SLC_BLOCK = 64
SLC_TOPN = 16
WINDOW = 512
FFN_HIDDEN = 2816
RMS_EPS = 1e-6
SEL_BIG = 1e9

M_QK = MLSTM_HEADS * MLSTM_DQK
M_V = MLSTM_HEADS * MLSTM_DV
N_Q = NSA_HEADS * NSA_DH
N_KV = NSA_KV_GROUPS * NSA_DH
IN_SPLITS = (2 * M_QK, M_V, M_V, MLSTM_HEADS, MLSTM_HEADS, N_Q, 6 * N_KV, 3 * NSA_HEADS, D_MODEL, D_MODEL)

LANES = 128
SLC_SLOTS = 128
MASK_BIG = float(2.0 ** 100)
NEG_BIG = -1e30
VMEM_LIMIT = 56 * 1024 * 1024

_SEG = dict(zip(("m_qk", "m_v", "m_o", "m_i", "m_f", "n_q", "n_kv", "n_g", "gate_a", "gate_b"),
                zip(np.cumsum((0,) + IN_SPLITS[:-1]).tolist(), IN_SPLITS)))
_WIDE = ("m_qk", "m_v", "m_o", "n_q", "n_kv", "gate_a", "gate_b")
_NARROW = ("m_i", "m_f", "n_g")
_NARROW_W = LANES
GATE_COL0 = 2 * MLSTM_HEADS


def _cparams(sem):
    return pltpu.CompilerParams(dimension_semantics=sem, vmem_limit_bytes=VMEM_LIMIT)


def _sigmoid(x):
    return 1.0 / (1.0 + jnp.exp(-x))


def _log_sigmoid(x):
    return jnp.minimum(x, 0.0) - jnp.log(1.0 + jnp.exp(-jnp.abs(x)))


def _rms(x, g):
    ms = jnp.mean(x * x, axis=-1, keepdims=True)
    return x * lax.rsqrt(ms + RMS_EPS) * g


def _dot_nt(a, b):
    return lax.dot_general(a, b, (((1,), (1,)), ((), ())), preferred_element_type=F32)


def _dot(a, b):
    return jnp.dot(a, b, preferred_element_type=F32)


def _inproj_kernel(x_ref, g_ref, w_ref, b_ref, *out_refs, widths):
    h = _rms(x_ref[...], g_ref[...]).astype(BF16)
    off = 0
    for o_ref, width in zip(out_refs, widths):
        acc = _dot(h, w_ref[:, off:off + width]) + b_ref[:, off:off + width]
        o_ref[...] = acc.astype(o_ref.dtype)
        off += width


def _inproj(x2d, norm_g, w_in, b_in, tm):
    T = x2d.shape[0]
    cols = [w_in[:, _SEG[n][0]:_SEG[n][0] + _SEG[n][1]] for n in _WIDE]
    bias = [b_in[_SEG[n][0]:_SEG[n][0] + _SEG[n][1]] for n in _WIDE]
    narrow_w = jnp.concatenate([w_in[:, _SEG[n][0]:_SEG[n][0] + _SEG[n][1]] for n in _NARROW], axis=1)
    narrow_b = jnp.concatenate([b_in[_SEG[n][0]:_SEG[n][0] + _SEG[n][1]] for n in _NARROW])
    pad = _NARROW_W - narrow_w.shape[1]
    cols.append(jnp.pad(narrow_w, ((0, 0), (0, pad))))
    bias.append(jnp.pad(narrow_b, (0, pad)))
    w = jnp.concatenate(cols, axis=1).astype(BF16)
    b = jnp.concatenate(bias)[None, :].astype(F32)
    widths = tuple(_SEG[n][1] for n in _WIDE) + (_NARROW_W,)
    dtypes = (BF16,) * len(_WIDE) + (F32,)
    n_tot = w.shape[1]
    return pl.pallas_call(
        functools.partial(_inproj_kernel, widths=widths),
        grid=(T // tm,),
        in_specs=[
            pl.BlockSpec((tm, D_MODEL), lambda i: (i, 0)),
            pl.BlockSpec((1, D_MODEL), lambda i: (0, 0)),
            pl.BlockSpec((D_MODEL, n_tot), lambda i: (0, 0)),
            pl.BlockSpec((1, n_tot), lambda i: (0, 0)),
        ],
        out_specs=[pl.BlockSpec((tm, wd), lambda i: (i, 0)) for wd in widths],
        out_shape=[jax.ShapeDtypeStruct((T, wd), dt) for wd, dt in zip(widths, dtypes)],
        compiler_params=_cparams(("parallel",)),
        name="inproj",
    )(x2d, norm_g[None, :], w, b)


_HALO = 16


def _conv_kernel(x_ref, halo_ref, w_ref, b_ref, q_ref, k_ref):
    i = pl.program_id(1)
    ts = x_ref.shape[0]
    x = x_ref[...].astype(F32)
    halo = jnp.where(i > 0, halo_ref[...].astype(F32), 0.0)
    xx = jnp.concatenate([halo, x], axis=0)
    acc = x * w_ref[CONV_K - 1:CONV_K, :] + b_ref[...]
    for j in range(1, CONV_K):
        shifted = pltpu.roll(xx, j, 0)[_HALO:_HALO + ts]
        acc = acc + shifted * w_ref[CONV_K - 1 - j:CONV_K - j, :]
    y = acc * _sigmoid(acc)
    q_ref[...] = y[:, :M_QK].astype(BF16)
    k_ref[...] = (y[:, M_QK:] * (MLSTM_DQK ** -0.5)).astype(BF16)


def _conv(m_qk, conv_w, conv_b, ts):
    B, S, C = m_qk.shape
    per = ts // _HALO
    return pl.pallas_call(
        _conv_kernel,
        grid=(B, S // ts),
        in_specs=[
            pl.BlockSpec((None, ts, C), lambda b, i: (b, i, 0)),
            pl.BlockSpec((None, _HALO, C), lambda b, i: (b, jnp.maximum(i * per - 1, 0), 0)),
            pl.BlockSpec((CONV_K, C), lambda b, i: (0, 0)),
            pl.BlockSpec((1, C), lambda b, i: (0, 0)),
        ],
        out_specs=[pl.BlockSpec((None, ts, M_QK), lambda b, i: (b, i, 0))] * 2,
        out_shape=[jax.ShapeDtypeStruct((B, S, M_QK), BF16)] * 2,
        compiler_params=_cparams(("parallel", "parallel")),
        name="mlstm_conv",
    )(m_qk, m_qk, conv_w, conv_b[None, :])


def _mlstm_kernel(q_ref, k_ref, v_ref, o_ref, gr_ref, gc_ref, fb_ref, ng_ref, y_ref, c_scr, n_scr, m_scr):
    hd = pl.program_id(1)
    L = q_ref.shape[0]

    @pl.when(pl.program_id(2) == 0)
    def _():
        c_scr[...] = jnp.zeros_like(c_scr)
        n_scr[...] = jnp.zeros_like(n_scr)
        m_scr[...] = jnp.zeros_like(m_scr)

    q = q_ref[...]
    k = k_ref[...]
    v = v_ref[...]
    fb = fb_ref[:, 0:1]
    i_row = gr_ref[pl.ds(hd, 1), :]
    lf_row = _log_sigmoid(gr_ref[pl.ds(MLSTM_HEADS + hd, 1), :] + fb)
    i_col = gc_ref[:, 0:1]
    lf_col = _log_sigmoid(gc_ref[:, 1:2] + fb)
    m_prev = m_scr[...]

    r_t = lax.broadcasted_iota(jnp.int32, (L, L), 0)
    r_s = lax.broadcasted_iota(jnp.int32, (L, L), 1)
    causal = r_s <= r_t
    b_col = jnp.sum(jnp.where(causal, lf_row, 0.0), axis=1, keepdims=True)
    b_row = jnp.sum(jnp.where(r_t <= r_s, lf_col, 0.0), axis=0, keepdims=True)
    b_last = jnp.sum(lf_row, axis=1, keepdims=True)

    d_intra = jnp.where(causal, b_col - b_row + i_row, NEG_BIG)
    d_inter = b_col + m_prev
    m_t = jnp.maximum(d_inter, jnp.max(d_intra, axis=1, keepdims=True))
    w_intra = jnp.exp(d_intra - m_t)
    w_inter = jnp.exp(d_inter - m_t)
    s = _dot_nt(q, k) * w_intra
    c_state = c_scr[...]
    n_state = n_scr[...]
    num = _dot(s.astype(BF16), v) + w_inter * _dot_nt(q, c_state.astype(BF16))
    nq = jnp.sum(s, axis=1, keepdims=True) + w_inter * jnp.sum(q.astype(F32) * n_state, axis=1, keepdims=True)
    h = num / jnp.maximum(jnp.abs(nq), jnp.exp(-m_t))
    h = _rms(h, ng_ref[...])
    y_ref[...] = (_sigmoid(o_ref[...].astype(F32)) * h).astype(y_ref.dtype)

    d_state_col = b_last - b_col + i_col
    d_state_row = b_last - b_row + i_row
    m_new = jnp.maximum(b_last + m_prev, jnp.max(d_state_row, axis=1, keepdims=True))
    w_state = jnp.exp(d_state_col - m_new)
    decay = jnp.exp(b_last + m_prev - m_new)
    vw_t = (v.astype(F32) * w_state).T.astype(BF16)
    c_scr[...] = decay * c_state + _dot(vw_t, k)
    n_scr[...] = decay * n_state + jnp.sum(w_state * k.astype(F32), axis=0, keepdims=True)
    m_scr[...] = m_new


def _mlstm(q, k, v, o, gates_row, gates_col, f_bias, norm_g, L):
    B, S, _ = q.shape
    H, DK, DV = MLSTM_HEADS, MLSTM_DQK, MLSTM_DV
    fb = jnp.broadcast_to(f_bias.astype(F32)[:, None, None], (H, 1, LANES))
    return pl.pallas_call(
        _mlstm_kernel,
        grid=(B, H, S // L),
        in_specs=[
            pl.BlockSpec((None, L, DK), lambda b, h, c: (b, c, h)),
            pl.BlockSpec((None, L, DK), lambda b, h, c: (b, c, h)),
            pl.BlockSpec((None, L, DV), lambda b, h, c: (b, c, h)),
            pl.BlockSpec((None, L, DV), lambda b, h, c: (b, c, h)),
            pl.BlockSpec((None, 2 * H, L), lambda b, h, c: (b, 0, c)),
            pl.BlockSpec((None, None, L, 2), lambda b, h, c: (b, h, c, 0)),
            pl.BlockSpec((None, 1, LANES), lambda b, h, c: (h, 0, 0)),
            pl.BlockSpec((1, DV), lambda b, h, c: (0, h)),
        ],
        out_specs=pl.BlockSpec((None, L, DV), lambda b, h, c: (b, c, h)),
        out_shape=jax.ShapeDtypeStruct((B, S, M_V), BF16),
        scratch_shapes=[pltpu.VMEM((DV, DK), F32), pltpu.VMEM((1, DK), F32), pltpu.VMEM((1, 1), F32)],
        compiler_params=_cparams(("parallel", "parallel", "arbitrary")),
        name="mlstm",
    )(q, k, v, o, gates_row, gates_col, fb, norm_g[None, :])


def _compress_kernel(s_ref, pos_ref, w1_ref, w2_ref, out_ref, out_t_ref):
    n = s_ref.shape[0]
    half = w1_ref.shape[0] // 2
    strips = s_ref[...]
    w1 = w1_ref[...].astype(BF16)
    first = _dot(strips, w1[:half])
    second = _dot(strips, w1[half:])
    pos_term = _dot(jnp.broadcast_to(pos_ref[...], (8, 2 * half)).astype(BF16), w1)[0:1]
    hid = first + pltpu.roll(second, n - 1, 0) + pos_term
    hid = hid * _sigmoid(hid)
    out = _dot(hid.astype(BF16), w2_ref[...].astype(BF16))
    row = lax.broadcasted_iota(jnp.int32, out.shape, 0)
    out = jnp.where(row < n - 1, out, 0.0)
    out_ref[...] = out.astype(out_ref.dtype)
    wide = jnp.concatenate([out, jnp.zeros_like(out)], axis=1)
    out_t_ref[...] = wide.T[:NSA_DH].astype(out_t_ref.dtype)


def _compress(strips, pos, w1, w2):
    _, B, G, n, sw = strips.shape
    return pl.pallas_call(
        _compress_kernel,
        grid=(2, B, G),
        in_specs=[
            pl.BlockSpec((None, None, None, n, sw), lambda a, b, g: (a, b, g, 0, 0)),
            pl.BlockSpec((None, 1, 2 * sw), lambda a, b, g: (a, 0, 0)),
            pl.BlockSpec((None, 2 * sw, CMP_HIDDEN), lambda a, b, g: (a, 0, 0)),
            pl.BlockSpec((None, CMP_HIDDEN, NSA_DH), lambda a, b, g: (a, 0, 0)),
        ],
        out_specs=[pl.BlockSpec((None, None, None, n, NSA_DH), lambda a, b, g: (a, b, g, 0, 0)),
                   pl.BlockSpec((None, None, None, NSA_DH, n), lambda a, b, g: (a, b, g, 0, 0))],
        out_shape=[jax.ShapeDtypeStruct((2, B, G, n, NSA_DH), BF16),
                   jax.ShapeDtypeStruct((2, B, G, NSA_DH, n), BF16)],
        compiler_params=_cparams(("parallel", "parallel", "parallel")),
        name="nsa_compress",
    )(strips, pos, w1, w2)


_CMP_PER_SLC = SLC_BLOCK // CMP_STRIDE


def _select_kernel(q_ref, kc_ref, vct_ref, qt_ref, ocmp_ref, mask_ref, ps_scr):
    tq = q_ref.shape[0]
    ncp = kc_ref.shape[1]
    n_slc = ncp // _CMP_PER_SLC
    t0 = pl.program_id(1) * tq
    q_t = (q_ref[...].astype(F32) * (NSA_DH ** -0.5)).T.astype(BF16)
    qt_ref[...] = q_t
    t_row = t0 + lax.broadcasted_iota(jnp.int32, (1, tq), 1)
    c_col = lax.broadcasted_iota(jnp.int32, (ncp, 1), 0)
    visible = (c_col * CMP_STRIDE + (CMP_BLOCK - 1)) <= t_row
    any_visible = t_row >= CMP_BLOCK - 1

    blk = lax.broadcasted_iota(jnp.int32, (SLC_SLOTS, tq), 0)
    cur = (t0 + lax.broadcasted_iota(jnp.int32, (SLC_SLOTS, tq), 1)) // SLC_BLOCK
    forced = (blk == 0) | (blk == cur) | (blk == cur - 1)
    eligible = blk <= cur

    def scores(hd):
        return _dot(kc_ref[hd // NSA_HPG], q_t[hd * NSA_DH:(hd + 1) * NSA_DH])

    s_next = scores(0)
    for g in range(NSA_KV_GROUPS):
        vct = vct_ref[g]
        for hh in range(NSA_HPG):
            hd = g * NSA_HPG + hh
            s = jnp.where(visible, s_next, NEG_BIG)
            if hd + 1 < NSA_HEADS:
                s_next = scores(hd + 1)
            e = jnp.exp(s - jnp.max(s, axis=0, keepdims=True))
            inv = jnp.where(any_visible, 1.0 / jnp.sum(e, axis=0, keepdims=True), 0.0)
            p = e * inv
            ocmp_ref[hd * NSA_DH:(hd + 1) * NSA_DH, :] = _dot(vct, p.astype(BF16)).astype(ocmp_ref.dtype)
            for c in range(tq // LANES):
                if hh == 0:
                    ps_scr[c] = p[:, c * LANES:(c + 1) * LANES]
                else:
                    ps_scr[c] += p[:, c * LANES:(c + 1) * LANES]
        lanes = [jnp.concatenate([ps_scr[c, pl.ds(r, n_slc, stride=_CMP_PER_SLC), :]
                                  for c in range(tq // LANES)], axis=1) for r in range(_CMP_PER_SLC)]
        row = lax.broadcasted_iota(jnp.int32, (n_slc, tq), 0)
        before = jnp.where(row == 0, 0.0, pltpu.roll(lanes[3], 1, 0))
        imp = before + 2.0 * (lanes[0] + lanes[1] + lanes[2]) + lanes[3]
        if n_slc < SLC_SLOTS:
            imp = jnp.concatenate([imp, jnp.zeros((SLC_SLOTS - n_slc, tq), F32)], axis=0)
        score = jnp.where(forced, SEL_BIG, jnp.where(eligible, imp, -SEL_BIG))
        sel = jnp.zeros(score.shape, jnp.bool_)
        for _ in range(SLC_TOPN):
            best = jnp.max(score, axis=0, keepdims=True)
            first = jnp.min(jnp.where(score == best, blk, SLC_SLOTS), axis=0, keepdims=True)
            hit = blk == first
            sel = sel | hit
            score = jnp.where(hit, -jnp.inf, score)
        mask_ref[g] = jnp.where(sel, 0.0, -MASK_BIG).astype(mask_ref.dtype)


def _select(n_q, kc, vct, tq):
    B, S, _ = n_q.shape
    G = NSA_KV_GROUPS
    ncp = kc.shape[2]
    return pl.pallas_call(
        _select_kernel,
        grid=(B, S // tq),
        in_specs=[
            pl.BlockSpec((None, tq, N_Q), lambda b, i: (b, i, 0)),
            pl.BlockSpec((None, G, ncp, NSA_DH), lambda b, i: (b, 0, 0, 0)),
            pl.BlockSpec((None, G, NSA_DH, ncp), lambda b, i: (b, 0, 0, 0)),
        ],
        out_specs=[
            pl.BlockSpec((None, N_Q, tq), lambda b, i: (b, 0, i)),
            pl.BlockSpec((None, N_Q, tq), lambda b, i: (b, 0, i)),
            pl.BlockSpec((None, G, SLC_SLOTS, tq), lambda b, i: (b, 0, 0, i)),
        ],
        out_shape=[
            jax.ShapeDtypeStruct((B, N_Q, S), BF16),
            jax.ShapeDtypeStruct((B, N_Q, S), BF16),
            jax.ShapeDtypeStruct((B, G, SLC_SLOTS, S), BF16),
        ],
        scratch_shapes=[pltpu.VMEM((tq // LANES, ncp, LANES), F32)],
        compiler_params=_cparams(("parallel", "parallel")),
        name="nsa_select",
    )(n_q, kc, vct)


_TK = 256
_KAUG_W = 2 * LANES
_BOUND_ROW = SLC_SLOTS + NSA_DH
_AHEAD = 7
_VT_ROWS = 80


def _attend_kernel(qt_ref, mask_ref, ocmp_ref, gate_ref, kaug_ref, vt_ref, kw_ref, vwt_ref, y_ref,
                   qa_scr, m_scr, acc_scr, yt_scr):
    tq = qt_ref.shape[1]
    S = kaug_ref.shape[1]
    G = NSA_KV_GROUPS
    rows = NSA_HPG * tq
    q0 = pl.program_id(1) * tq
    t_q = q0 + lax.broadcasted_iota(jnp.int32, (1, tq), 1)
    n_full = q0 // _TK
    kd = pl.multiple_of(n_full * _TK, _TK)
    causal = (kd + lax.broadcasted_iota(jnp.int32, (_TK, 1), 0)) <= t_q

    for g in range(G):
        for hh in range(NSA_HPG):
            hd = g * NSA_HPG + hh
            cols = slice(hh * tq, (hh + 1) * tq)
            qa_scr[g, 0:SLC_SLOTS, cols] = mask_ref[g]
            qa_scr[g, SLC_SLOTS:SLC_SLOTS + NSA_DH, cols] = qt_ref[hd * NSA_DH:(hd + 1) * NSA_DH, :]
        qa_scr[g, SLC_SLOTS + NSA_DH:, :] = jnp.zeros((_KAUG_W - SLC_SLOTS - NSA_DH, rows), BF16)

    heads = [(g, hh) for g in range(G) for hh in range(NSA_HPG)]

    def scores(n, k0):
        g, hh = heads[n]
        return _dot(kaug_ref[g, pl.ds(k0, _TK), :], qa_scr[g, :, hh * tq:(hh + 1) * tq])

    def key_sweep(consume):
        def tiles(starts, diagonal):
            work = [(n, k0) for k0 in starts for n in range(len(heads))]
            pending = [scores(*w) for w in work[:_AHEAD]]
            for i, (n, k0) in enumerate(work):
                s = pending.pop(0)
                if i + _AHEAD < len(work):
                    pending.append(scores(*work[i + _AHEAD]))
                consume(*heads[n], k0, jnp.where(causal, s, -MASK_BIG) if diagonal else s)

        def pair(j, carry):
            k0 = pl.multiple_of(2 * j * _TK, _TK)
            tiles([k0, pl.multiple_of(k0 + _TK, _TK)], False)
            return carry

        lax.fori_loop(0, n_full // 2, pair, 0)

        @pl.when(n_full % 2 == 1)
        def _():
            tiles([pl.multiple_of(kd - _TK, _TK)], False)

        tiles([kd], True)

    m_scr[...] = jnp.full(m_scr.shape, NEG_BIG, F32)

    acc_scr[...] = jnp.zeros_like(acc_scr)

    def accumulate(g, hh, k0, s):
        cols = slice(hh * tq, (hh + 1) * tq)
        m_old = m_scr[g, :, cols]
        m_new = jnp.maximum(m_old, jnp.max(s, axis=0, keepdims=True))
        p = jnp.exp((s - m_new[0:1]).astype(BF16))
        acc_scr[g, :, cols] = (jnp.exp(m_old - m_new)[0:1] * acc_scr[g, :, cols]
                               + _dot(vt_ref[g, :, pl.ds(k0, _TK)], p))
        m_scr[g, :, cols] = m_new

    key_sweep(accumulate)

    gates_t = _sigmoid(gate_ref[...]).T
    wlen = min(tq + WINDOW, S)
    w0 = pl.multiple_of(jnp.maximum(q0 - WINDOW, 0), tq)
    kpos = w0 + lax.broadcasted_iota(jnp.int32, (wlen, 1), 0)
    band = (kpos <= t_q) & (kpos > t_q - WINDOW)

    def window_scores(g, hh):
        hd = g * NSA_HPG + hh
        return _dot(kw_ref[g, pl.ds(w0, wlen), :], qt_ref[hd * NSA_DH:(hd + 1) * NSA_DH, :])

    s_next = window_scores(*heads[0])
    for i, (g, hh) in enumerate(heads):
        s = jnp.where(band, s_next, NEG_BIG)
        if i + 1 < len(heads):
            s_next = window_scores(*heads[i + 1])
        hd = g * NSA_HPG + hh
        rows_h = slice(hd * NSA_DH, (hd + 1) * NSA_DH)
        e = jnp.exp(s - jnp.max(s, axis=0, keepdims=True))
        ow = _dot(vwt_ref[g, :, pl.ds(w0, wlen)], e.astype(BF16))
        o_win = ow[:NSA_DH] / ow[NSA_DH:NSA_DH + 1]
        acc = acc_scr[g, :, hh * tq:(hh + 1) * tq]
        o_slc = acc[:NSA_DH] / acc[NSA_DH:NSA_DH + 1]
        gr = GATE_COL0 + 3 * hd
        yt_scr[rows_h, :] = (gates_t[gr:gr + 1] * ocmp_ref[rows_h, :].astype(F32)
                             + gates_t[gr + 1:gr + 2] * o_slc
                             + gates_t[gr + 2:gr + 3] * o_win)
    y_ref[...] = yt_scr[...].T.astype(y_ref.dtype)


def _attend(q_t, mask_t, ocmp_t, narrow, kaug, v_t, kw, vw_t, tq):
    B, _, S = q_t.shape
    G = NSA_KV_GROUPS
    rows = NSA_HPG * tq
    full = lambda b, i: (b, 0, 0, 0)
    return pl.pallas_call(
        _attend_kernel,
        grid=(B, S // tq),
        in_specs=[
            pl.BlockSpec((None, N_Q, tq), lambda b, i: (b, 0, i)),
            pl.BlockSpec((None, G, SLC_SLOTS, tq), lambda b, i: (b, 0, 0, i)),
            pl.BlockSpec((None, N_Q, tq), lambda b, i: (b, 0, i)),
            pl.BlockSpec((None, tq, _NARROW_W), lambda b, i: (b, i, 0)),
            pl.BlockSpec((None, G, S, _KAUG_W), full),
            pl.BlockSpec((None, G, _VT_ROWS, S), full),
            pl.BlockSpec((None, G, S, NSA_DH), full),
            pl.BlockSpec((None, G, _VT_ROWS, S), full),
        ],
        out_specs=pl.BlockSpec((None, tq, N_Q), lambda b, i: (b, i, 0)),
        out_shape=jax.ShapeDtypeStruct((B, S, N_Q), BF16),
        scratch_shapes=[
            pltpu.VMEM((G, _KAUG_W, rows), BF16),
            pltpu.VMEM((G, 8, rows), F32),
            pltpu.VMEM((G, _VT_ROWS, rows), F32),
            pltpu.VMEM((N_Q, tq), F32),
        ],
        compiler_params=_cparams(("parallel", "arbitrary")),
        name="nsa_attend",
    )(q_t, mask_t, ocmp_t, narrow, kaug, v_t, kw, vw_t)


def _values_t(v):
    B, G, S, dh = v.shape
    tail = jnp.zeros((_VT_ROWS - dh, S), v.dtype).at[0].set(1.0)
    return jnp.concatenate([v.transpose(0, 1, 3, 2), jnp.broadcast_to(tail, (B, G, _VT_ROWS - dh, S))], axis=2)


def _merge_kernel(x_ref, ya_ref, yb_ref, ga_ref, gb_ref, wa_ref, wb_ref, wo_ref, g2_ref, x1_ref, h2_ref):
    a = _sigmoid(ga_ref[...].astype(F32)) * _dot(ya_ref[...], wa_ref[...])
    b = _sigmoid(gb_ref[...].astype(F32)) * _dot(yb_ref[...], wb_ref[...])
    x1 = x_ref[...] + _dot((a + b).astype(BF16), wo_ref[...])
    x1_ref[...] = x1
    h2_ref[...] = _rms(x1, g2_ref[...]).astype(h2_ref.dtype)


def _merge(x2d, ya, yb, ga, gb, wa, wb, wo, g2, tm):
    T = x2d.shape[0]
    row = lambda w: pl.BlockSpec((tm, w), lambda i: (i, 0))
    const = lambda r, c: pl.BlockSpec((r, c), lambda i: (0, 0))
    return pl.pallas_call(
        _merge_kernel,
        grid=(T // tm,),
        in_specs=[row(D_MODEL), row(M_V), row(N_Q), row(D_MODEL), row(D_MODEL),
                  const(M_V, D_MODEL), const(N_Q, D_MODEL), const(D_MODEL, D_MODEL), const(1, D_MODEL)],
        out_specs=[row(D_MODEL), row(D_MODEL)],
        out_shape=[jax.ShapeDtypeStruct((T, D_MODEL), F32), jax.ShapeDtypeStruct((T, D_MODEL), BF16)],
        compiler_params=_cparams(("parallel",)),
        name="merge",
    )(x2d, ya, yb, ga, gb, wa.astype(BF16), wb.astype(BF16), wo.astype(BF16), g2[None, :])


_FFN_CHUNK = 256


def _ffn_kernel(x1_ref, h2_ref, wg_ref, wu_ref, wd_ref, gf_ref, out_ref):
    h = h2_ref[...]
    acc = x1_ref[...]
    for c in range(FFN_HIDDEN // _FFN_CHUNK):
        sl = slice(c * _FFN_CHUNK, (c + 1) * _FFN_CHUNK)
        gate = _dot(h, wg_ref[:, sl])
        up = _dot(h, wu_ref[:, sl])
        acc = acc + _dot((gate * _sigmoid(gate) * up).astype(BF16), wd_ref[sl, :])
    out_ref[...] = _rms(acc, gf_ref[...])


def _ffn(x1, h2, wg, wu, wd, gf, tm):
    T = x1.shape[0]
    row = lambda: pl.BlockSpec((tm, D_MODEL), lambda i: (i, 0))
    const = lambda r, c: pl.BlockSpec((r, c), lambda i: (0, 0))
    return pl.pallas_call(
        _ffn_kernel,
        grid=(T // tm,),
        in_specs=[row(), row(), const(D_MODEL, FFN_HIDDEN), const(D_MODEL, FFN_HIDDEN),
                  const(FFN_HIDDEN, D_MODEL), const(1, D_MODEL)],
        out_specs=row(),
        out_shape=jax.ShapeDtypeStruct((T, D_MODEL), F32),
        compiler_params=_cparams(("parallel",)),
        name="ffn",
    )(x1, h2, wg.astype(BF16), wu.astype(BF16), wd.astype(BF16), gf[None, :])


def _layer(x, norm1_g, w_in, b_in, f_bias, conv_w, conv_b, mlstm_norm_g, cmp_k_pos, cmp_k_w1, cmp_k_w2,
           cmp_v_pos, cmp_v_w1, cmp_v_w2, w_branch_a, w_branch_b, w_out, norm2_g):
    B, S, D = x.shape
    T = B * S
    G, DH, H = NSA_KV_GROUPS, NSA_DH, MLSTM_HEADS
    x2d = x.reshape(T, D)
    tm = min(256, T)

    m_qk, m_v, m_o, n_q, n_kv, gate_a, gate_b, narrow = _inproj(x2d, norm1_g, w_in, b_in, tm)

    q, k = _conv(m_qk.reshape(B, S, 2 * M_QK), conv_w, conv_b, min(512, S))
    gates = narrow[:, :2 * H].reshape(B, S, 2 * H)
    gates_row = gates.transpose(0, 2, 1)
    gates_col = gates.reshape(B, S, 2, H).transpose(0, 3, 1, 2)
    y_a = _mlstm(q, k, m_v.reshape(B, S, M_V), m_o.reshape(B, S, M_V), gates_row, gates_col,
                 f_bias, mlstm_norm_g, min(256, S))

    kv = n_kv.reshape(B, S, 6, G, DH).transpose(2, 0, 3, 1, 4)
    ncp = S // CMP_STRIDE
    strips = kv[0:2].reshape(2, B, G, ncp, CMP_STRIDE * DH)
    pos = jnp.stack([cmp_k_pos, cmp_v_pos]).reshape(2, 1, CMP_BLOCK * DH)
    cmp, cmp_t = _compress(strips, pos, jnp.stack([cmp_k_w1, cmp_v_w1]), jnp.stack([cmp_k_w2, cmp_v_w2]))
    tq = min(256, S)
    q_t, ocmp_t, mask_t = _select(n_q.reshape(B, S, N_Q), cmp[0], cmp_t[1], tq)
    onehot = jax.nn.one_hot(jnp.arange(S) // SLC_BLOCK, SLC_SLOTS, dtype=BF16)
    kaug = jnp.concatenate([jnp.broadcast_to(onehot, (B, G, S, SLC_SLOTS)), kv[2],
                            jnp.ones((B, G, S, 1), BF16),
                            jnp.zeros((B, G, S, _KAUG_W - SLC_SLOTS - DH - 1), BF16)], axis=-1)
    y_b = _attend(q_t, mask_t, ocmp_t, narrow.reshape(B, S, _NARROW_W), kaug, _values_t(kv[3]), kv[4],
                  _values_t(kv[5]), tq)

    return _merge(x2d, y_a.reshape(T, M_V), y_b.reshape(T, N_Q), gate_a, gate_b,
                  w_branch_a, w_branch_b, w_out, norm2_g, tm)


def kernel(x, norm1_g, w_in, b_in, f_bias, conv_w, conv_b, mlstm_norm_g, cmp_k_pos, cmp_k_w1, cmp_k_w2,
           cmp_v_pos, cmp_v_w1, cmp_v_w2, w_branch_a, w_branch_b, w_out, norm2_g, w_ffn_gate, w_ffn_up,
           w_ffn_down, norm_f_g):
    B, S, D = x.shape
    depth = w_in.shape[0]
    assert depth == 1, "the fused final norm assumes a single layer"
    x1, h2 = _layer(x, norm1_g[0], w_in[0], b_in[0], f_bias[0], conv_w[0], conv_b[0], mlstm_norm_g[0],
                    cmp_k_pos[0], cmp_k_w1[0], cmp_k_w2[0], cmp_v_pos[0], cmp_v_w1[0], cmp_v_w2[0],
                    w_branch_a[0], w_branch_b[0], w_out[0], norm2_g[0])
    out = _ffn(x1, h2, w_ffn_gate[0], w_ffn_up[0], w_ffn_down[0], norm_f_g, min(256, B * S))
    return out.reshape(B, S, D)
```

```python
import functools

import numpy as np
import jax
import jax.numpy as jnp
from jax import lax
from jax.experimental import pallas as pl
from jax.experimental.pallas import tpu as pltpu

F32 = jnp.float32
BF16 = jnp.bfloat16

D_MODEL = 1024
MLSTM_HEADS = 4
MLSTM_DV = D_MODEL // MLSTM_HEADS
MLSTM_DQK = MLSTM_DV // 2
CONV_K = 4
NSA_DH = 64
NSA_HEADS = (D_MODEL // 2) // NSA_DH
NSA_KV_GROUPS = 2
NSA_HPG = NSA_HEADS // NSA_KV_GROUPS
CMP_BLOCK = 32
CMP_STRIDE = 16
CMP_HIDDEN = 256
SLC_BLOCK = 64
SLC_TOPN = 16
WINDOW = 512
FFN_HIDDEN = 2816
RMS_EPS = 1e-6
SEL_BIG = 1e9

M_QK = MLSTM_HEADS * MLSTM_DQK
M_V = MLSTM_HEADS * MLSTM_DV
N_Q = NSA_HEADS * NSA_DH
N_KV = NSA_KV_GROUPS * NSA_DH
IN_SPLITS = (2 * M_QK, M_V, M_V, MLSTM_HEADS, MLSTM_HEADS, N_Q, 6 * N_KV, 3 * NSA_HEADS, D_MODEL, D_MODEL)

LANES = 128
SLC_SLOTS = 128
MASK_BIG = float(2.0 ** 100)
NEG_BIG = -1e30
VMEM_LIMIT = 56 * 1024 * 1024
assert N_KV == LANES, "both KV groups are packed into one 128-lane key row"

_SEG = dict(zip(("m_qk", "m_v", "m_o", "m_i", "m_f", "n_q", "n_kv", "n_g", "gate_a", "gate_b"),
                zip(np.cumsum((0,) + IN_SPLITS[:-1]).tolist(), IN_SPLITS)))
_WIDE = ("m_qk", "m_v", "m_o", "n_q", "n_kv", "gate_a", "gate_b")
_NARROW = ("m_i", "m_f", "n_g")
_NARROW_W = LANES
GATE_COL0 = 2 * MLSTM_HEADS
_VT_ROWS = 80


def _cparams(sem):
    return pltpu.CompilerParams(dimension_semantics=sem, vmem_limit_bytes=VMEM_LIMIT)


def _sigmoid(x):
    return 1.0 / (1.0 + jnp.exp(-x))


def _log_sigmoid(x):
    return jnp.minimum(x, 0.0) - jnp.log(1.0 + jnp.exp(-jnp.abs(x)))


def _rms(x, g):
    ms = jnp.mean(x * x, axis=-1, keepdims=True)
    return x * lax.rsqrt(ms + RMS_EPS) * g


def _dot_nt(a, b):
    return lax.dot_general(a, b, (((1,), (1,)), ((), ())), preferred_element_type=F32)


def _dot(a, b):
    return jnp.dot(a, b, preferred_element_type=F32)


def _values_t(v):
    v_t = v.T
    row = lax.broadcasted_iota(jnp.int32, (_VT_ROWS - NSA_DH, v.shape[0]), 0)
    tail = jnp.where(row == 0, 1.0, 0.0)
    return [jnp.concatenate([v_t[g * NSA_DH:(g + 1) * NSA_DH], tail], axis=0).astype(BF16)
            for g in range(NSA_KV_GROUPS)]


def _inproj_kernel(x_ref, g_ref, w_ref, b_ref, qk_ref, v_ref, o_ref, nq_ref, ga_ref, gb_ref, narrow_ref,
                   narrow_t_ref, cmp_ref, kaug_ref, vst_ref, kw_ref, vwt_ref, *, offs, tiles_per_seq):
    tm = x_ref.shape[0]
    h = _rms(x_ref[...], g_ref[...]).astype(BF16)

    def seg(off, width):
        return _dot(h, w_ref[:, off:off + width]) + b_ref[:, off:off + width]

    qk_ref[...] = seg(offs["m_qk"], 2 * M_QK).astype(BF16)
    v_ref[...] = seg(offs["m_v"], M_V).astype(BF16)
    o_ref[...] = seg(offs["m_o"], M_V).astype(BF16)
    nq_ref[...] = seg(offs["n_q"], N_Q).astype(BF16)
    ga_ref[...] = seg(offs["gate_a"], D_MODEL).astype(BF16)
    gb_ref[...] = seg(offs["gate_b"], D_MODEL).astype(BF16)
    narrow = seg(offs["narrow"], _NARROW_W)
    narrow_ref[...] = narrow
    narrow_t_ref[...] = narrow.T

    kv0 = offs["n_kv"]
    cmp_ref[...] = seg(kv0, 2 * N_KV)
    pos = (pl.program_id(0) % tiles_per_seq) * tm + lax.broadcasted_iota(jnp.int32, (tm, SLC_SLOTS), 0)
    onehot = (pos // SLC_BLOCK == lax.broadcasted_iota(jnp.int32, (tm, SLC_SLOTS), 1)).astype(BF16)
    kaug_ref[...] = jnp.concatenate([onehot, seg(kv0 + 2 * N_KV, N_KV).astype(BF16)], axis=1)
    for g, vt in enumerate(_values_t(seg(kv0 + 3 * N_KV, N_KV))):
        vst_ref[g] = vt
    kw_ref[...] = seg(kv0 + 4 * N_KV, N_KV).astype(BF16)
    for g, vt in enumerate(_values_t(seg(kv0 + 5 * N_KV, N_KV))):
        vwt_ref[g] = vt


def _inproj(x, norm_g, w_in, b_in, tm):
    B, S, _ = x.shape
    T = B * S
    G = NSA_KV_GROUPS
    cols = [w_in[:, _SEG[n][0]:_SEG[n][0] + _SEG[n][1]] for n in _WIDE]
    bias = [b_in[_SEG[n][0]:_SEG[n][0] + _SEG[n][1]] for n in _WIDE]
    narrow_w = jnp.concatenate([w_in[:, _SEG[n][0]:_SEG[n][0] + _SEG[n][1]] for n in _NARROW], axis=1)
    narrow_b = jnp.concatenate([b_in[_SEG[n][0]:_SEG[n][0] + _SEG[n][1]] for n in _NARROW])
    pad = _NARROW_W - narrow_w.shape[1]
    cols.append(jnp.pad(narrow_w, ((0, 0), (0, pad))))
    bias.append(jnp.pad(narrow_b, (0, pad)))
    w = jnp.concatenate(cols, axis=1).astype(BF16)
    b = jnp.concatenate(bias)[None, :].astype(F32)
    widths = [_SEG[n][1] for n in _WIDE] + [_NARROW_W]
    offs = dict(zip(_WIDE + ("narrow",), np.cumsum([0] + widths[:-1]).tolist()))
    n_tot = w.shape[1]
    tps = S // tm

    def rows(width, dtype):
        return pl.BlockSpec((tm, width), lambda i: (i, 0)), jax.ShapeDtypeStruct((T, width), dtype)

    def rows_t(lead, dtype):
        nd = len(lead)
        return (pl.BlockSpec((None,) + lead + (tm,), lambda i: (i // tps,) + (0,) * nd + (i % tps,)),
                jax.ShapeDtypeStruct((B,) + lead + (S,), dtype))

    outs = [rows(2 * M_QK, BF16), rows(M_V, BF16), rows(M_V, BF16), rows(N_Q, BF16), rows(D_MODEL, BF16),
            rows(D_MODEL, BF16), rows(_NARROW_W, F32), rows_t((_NARROW_W,), F32), rows(2 * N_KV, F32),
            rows(SLC_SLOTS + N_KV, BF16), rows_t((G, _VT_ROWS), BF16), rows(N_KV, BF16),
            rows_t((G, _VT_ROWS), BF16)]
    return pl.pallas_call(
        functools.partial(_inproj_kernel, offs=offs, tiles_per_seq=tps),
        grid=(T // tm,),
        in_specs=[
            pl.BlockSpec((tm, D_MODEL), lambda i: (i, 0)),
            pl.BlockSpec((1, D_MODEL), lambda i: (0, 0)),
            pl.BlockSpec((D_MODEL, n_tot), lambda i: (0, 0)),
            pl.BlockSpec((1, n_tot), lambda i: (0, 0)),
        ],
        out_specs=[o[0] for o in outs],
        out_shape=[o[1] for o in outs],
        compiler_params=_cparams(("parallel",)),
        name="inproj",
    )(x.reshape(T, D_MODEL), norm_g[None, :], w, b)


_HALO = 16


def _conv_kernel(x_ref, halo_ref, w_ref, b_ref, q_ref, k_ref):
    i = pl.program_id(1)
    ts = x_ref.shape[0]
    x = x_ref[...].astype(F32)
    halo = jnp.where(i > 0, halo_ref[...].astype(F32), 0.0)
    xx = jnp.concatenate([halo, x], axis=0)
    acc = x * w_ref[CONV_K - 1:CONV_K, :] + b_ref[...]
    for j in range(1, CONV_K):
        shifted = pltpu.roll(xx, j, 0)[_HALO:_HALO + ts]
        acc = acc + shifted * w_ref[CONV_K - 1 - j:CONV_K - j, :]
    y = acc * _sigmoid(acc)
    q_ref[...] = y[:, :M_QK].astype(BF16)
    k_ref[...] = (y[:, M_QK:] * (MLSTM_DQK ** -0.5)).astype(BF16)


def _conv(m_qk, conv_w, conv_b, ts):
    B, S, C = m_qk.shape
    per = ts // _HALO
    return pl.pallas_call(
        _conv_kernel,
        grid=(B, S // ts),
        in_specs=[
            pl.BlockSpec((None, ts, C), lambda b, i: (b, i, 0)),
            pl.BlockSpec((None, _HALO, C), lambda b, i: (b, jnp.maximum(i * per - 1, 0), 0)),
            pl.BlockSpec((CONV_K, C), lambda b, i: (0, 0)),
            pl.BlockSpec((1, C), lambda b, i: (0, 0)),
        ],
        out_specs=[pl.BlockSpec((None, ts, M_QK), lambda b, i: (b, i, 0))] * 2,
        out_shape=[jax.ShapeDtypeStruct((B, S, M_QK), BF16)] * 2,
        compiler_params=_cparams(("parallel", "parallel")),
        name="mlstm_conv",
    )(m_qk, m_qk, conv_w, conv_b[None, :])


def _mlstm_kernel(q_ref, k_ref, v_ref, o_ref, gr_ref, gc_ref, fb_ref, ng_ref, y_ref, c_scr, n_scr, m_scr):
    hd = pl.program_id(1)
    L = q_ref.shape[0]

    @pl.when(pl.program_id(2) == 0)
    def _():
        c_scr[...] = jnp.zeros_like(c_scr)
        n_scr[...] = jnp.zeros_like(n_scr)
        m_scr[...] = jnp.zeros_like(m_scr)

    q = q_ref[...]
    k = k_ref[...]
    v = v_ref[...]
    fb = fb_ref[:, 0:1]
    i_row = gr_ref[pl.ds(hd, 1), :]
    lf_row = _log_sigmoid(gr_ref[pl.ds(MLSTM_HEADS + hd, 1), :] + fb)
    gates = gc_ref[...]
    lane = lax.broadcasted_iota(jnp.int32, gates.shape, 1)
    i_col = jnp.sum(jnp.where(lane == hd, gates, 0.0), axis=1, keepdims=True)
    lf_col = _log_sigmoid(jnp.sum(jnp.where(lane == MLSTM_HEADS + hd, gates, 0.0), axis=1, keepdims=True) + fb)
    m_prev = m_scr[...]

    r_t = lax.broadcasted_iota(jnp.int32, (L, L), 0)
    r_s = lax.broadcasted_iota(jnp.int32, (L, L), 1)
    causal = r_s <= r_t
    b_col = jnp.sum(jnp.where(causal, lf_row, 0.0), axis=1, keepdims=True)
    b_row = jnp.sum(jnp.where(r_t <= r_s, lf_col, 0.0), axis=0, keepdims=True)
    b_last = jnp.sum(lf_row, axis=1, keepdims=True)

    d_intra = jnp.where(causal, b_col - b_row + i_row, NEG_BIG)
    d_inter = b_col + m_prev
    m_t = jnp.maximum(d_inter, jnp.max(d_intra, axis=1, keepdims=True))
    w_intra = jnp.exp(d_intra - m_t)
    w_inter = jnp.exp(d_inter - m_t)
    s = _dot_nt(q, k) * w_intra
    c_state = c_scr[...]
    n_state = n_scr[...]
    num = _dot(s.astype(BF16), v) + w_inter * _dot_nt(q, c_state.astype(BF16))
    nq = jnp.sum(s, axis=1, keepdims=True) + w_inter * jnp.sum(q.astype(F32) * n_state, axis=1, keepdims=True)
    h = num / jnp.maximum(jnp.abs(nq), jnp.exp(-m_t))
    h = _rms(h, ng_ref[...])
    y_ref[...] = (_sigmoid(o_ref[...].astype(F32)) * h).astype(y_ref.dtype)

    d_state_col = b_last - b_col + i_col
    d_state_row = b_last - b_row + i_row
    m_new = jnp.maximum(b_last + m_prev, jnp.max(d_state_row, axis=1, keepdims=True))
    w_state = jnp.exp(d_state_col - m_new)
    decay = jnp.exp(b_last + m_prev - m_new)
    vw_t = (v.astype(F32) * w_state).T.astype(BF16)
    c_scr[...] = decay * c_state + _dot(vw_t, k)
    n_scr[...] = decay * n_state + jnp.sum(w_state * k.astype(F32), axis=0, keepdims=True)
    m_scr[...] = m_new


def _mlstm(q, k, v, o, narrow, narrow_t, f_bias, norm_g, L):
    B, S, _ = q.shape
    H, DK, DV = MLSTM_HEADS, MLSTM_DQK, MLSTM_DV
    fb = jnp.broadcast_to(f_bias.astype(F32)[:, None, None], (H, 1, LANES))
    return pl.pallas_call(
        _mlstm_kernel,
        grid=(B, H, S // L),
        in_specs=[
            pl.BlockSpec((None, L, DK), lambda b, h, c: (b, c, h)),
            pl.BlockSpec((None, L, DK), lambda b, h, c: (b, c, h)),
            pl.BlockSpec((None, L, DV), lambda b, h, c: (b, c, h)),
            pl.BlockSpec((None, L, DV), lambda b, h, c: (b, c, h)),
            pl.BlockSpec((None, 2 * H, L), lambda b, h, c: (b, 0, c)),
            pl.BlockSpec((None, L, _NARROW_W), lambda b, h, c: (b, c, 0)),
            pl.BlockSpec((None, 1, LANES), lambda b, h, c: (h, 0, 0)),
            pl.BlockSpec((1, DV), lambda b, h, c: (0, h)),
        ],
        out_specs=pl.BlockSpec((None, L, DV), lambda b, h, c: (b, c, h)),
        out_shape=jax.ShapeDtypeStruct((B, S, M_V), BF16),
        scratch_shapes=[pltpu.VMEM((DV, DK), F32), pltpu.VMEM((1, DK), F32), pltpu.VMEM((1, 1), F32)],
        compiler_params=_cparams(("parallel", "parallel", "arbitrary")),
        name="mlstm",
    )(q, k, v, o, narrow_t, narrow, fb, norm_g[None, :])


def _compress_kernel(x_ref, pos_ref, w1_ref, w2_ref, out_ref, out_t_ref, x_scr):
    S = x_ref.shape[0]
    n = S // CMP_STRIDE
    x_scr[0:S] = x_ref[...]
    x_scr[S:] = jnp.zeros((CMP_STRIDE, LANES), F32)
    acc = jnp.zeros((n, w1_ref.shape[2]), F32)
    for l in range(CMP_BLOCK):
        tok = x_scr[pl.ds(l, n, stride=CMP_STRIDE), :] + pos_ref[l:l + 1, :]
        acc = acc + _dot(tok.astype(BF16), w1_ref[l])
    hid = acc * _sigmoid(acc)
    out = _dot(hid.astype(BF16), w2_ref[...])
    row = lax.broadcasted_iota(jnp.int32, out.shape, 0)
    out = jnp.where(row < n - 1, out, 0.0)
    out_ref[...] = out.astype(out_ref.dtype)
    out_t_ref[...] = out.T.astype(out_t_ref.dtype)


def _block_diag2(w):
    z = jnp.zeros_like(w)
    return jnp.concatenate([jnp.concatenate([w, z], axis=-1), jnp.concatenate([z, w], axis=-1)], axis=-2)


def _compress(cmp_kv, pos, w1, w2):
    B, S, _ = cmp_kv.shape
    n = S // CMP_STRIDE
    pos2 = jnp.concatenate([pos, pos], axis=-1)
    w1b = _block_diag2(w1.reshape(2, CMP_BLOCK, NSA_DH, CMP_HIDDEN)).astype(BF16)
    w2b = _block_diag2(w2).astype(BF16)
    return pl.pallas_call(
        _compress_kernel,
        grid=(B, 2),
        in_specs=[
            pl.BlockSpec((None, S, LANES), lambda b, a: (b, 0, a)),
            pl.BlockSpec((None, CMP_BLOCK, LANES), lambda b, a: (a, 0, 0)),
            pl.BlockSpec((None, CMP_BLOCK, LANES, 2 * CMP_HIDDEN), lambda b, a: (a, 0, 0, 0)),
            pl.BlockSpec((None, 2 * CMP_HIDDEN, LANES), lambda b, a: (a, 0, 0)),
        ],
        out_specs=[pl.BlockSpec((None, None, n, LANES), lambda b, a: (b, a, 0, 0)),
                   pl.BlockSpec((None, None, LANES, n), lambda b, a: (b, a, 0, 0))],
        out_shape=[jax.ShapeDtypeStruct((B, 2, n, LANES), BF16),
                   jax.ShapeDtypeStruct((B, 2, LANES, n), BF16)],
        scratch_shapes=[pltpu.VMEM((S + CMP_STRIDE, LANES), F32)],
        compiler_params=_cparams(("parallel", "parallel")),
        name="nsa_compress",
    )(cmp_kv, pos2, w1b, w2b)


_CMP_PER_SLC = SLC_BLOCK // CMP_STRIDE


def _group_rows(q_h, g):
    z = jnp.zeros_like(q_h)
    return jnp.concatenate([q_h, z] if g == 0 else [z, q_h], axis=0)


def _select_kernel(q_ref, kc_ref, vct_ref, qt_ref, ocmp_ref, mask_ref, ps_scr):
    tq = q_ref.shape[0]
    ncp = kc_ref.shape[0]
    n_slc = ncp // _CMP_PER_SLC
    t0 = pl.program_id(1) * tq
    q_t = (q_ref[...].astype(F32) * (NSA_DH ** -0.5)).T.astype(BF16)
    qt_ref[...] = q_t
    t_row = t0 + lax.broadcasted_iota(jnp.int32, (1, tq), 1)
    c_col = lax.broadcasted_iota(jnp.int32, (ncp, 1), 0)
    visible = (c_col * CMP_STRIDE + (CMP_BLOCK - 1)) <= t_row
    any_visible = t_row >= CMP_BLOCK - 1

    blk = lax.broadcasted_iota(jnp.int32, (SLC_SLOTS, tq), 0)
    cur = (t0 + lax.broadcasted_iota(jnp.int32, (SLC_SLOTS, tq), 1)) // SLC_BLOCK
    forced = (blk == 0) | (blk == cur) | (blk == cur - 1)
    eligible = blk <= cur

    def scores(hd):
        return _dot(kc_ref[...], _group_rows(q_t[hd * NSA_DH:(hd + 1) * NSA_DH], hd // NSA_HPG))

    s_next = scores(0)
    for g in range(NSA_KV_GROUPS):
        vct = vct_ref[g * NSA_DH:(g + 1) * NSA_DH, :]
        for hh in range(NSA_HPG):
            hd = g * NSA_HPG + hh
            s = jnp.where(visible, s_next, NEG_BIG)
            if hd + 1 < NSA_HEADS:
                s_next = scores(hd + 1)
            e = jnp.exp(s - jnp.max(s, axis=0, keepdims=True))
            inv = jnp.where(any_visible, 1.0 / jnp.sum(e, axis=0, keepdims=True), 0.0)
            p = e * inv
            ocmp_ref[hd * NSA_DH:(hd + 1) * NSA_DH, :] = _dot(vct, p.astype(BF16)).astype(ocmp_ref.dtype)
            for c in range(tq // LANES):
                if hh == 0:
                    ps_scr[c] = p[:, c * LANES:(c + 1) * LANES]
                else:
                    ps_scr[c] += p[:, c * LANES:(c + 1) * LANES]
        lanes = [jnp.concatenate([ps_scr[c, pl.ds(r, n_slc, stride=_CMP_PER_SLC), :]
                                  for c in range(tq // LANES)], axis=1) for r in range(_CMP_PER_SLC)]
        row = lax.broadcasted_iota(jnp.int32, (n_slc, tq), 0)
        before = jnp.where(row == 0, 0.0, pltpu.roll(lanes[3], 1, 0))
        imp = before + 2.0 * (lanes[0] + lanes[1] + lanes[2]) + lanes[3]
        if n_slc < SLC_SLOTS:
            imp = jnp.concatenate([imp, jnp.zeros((SLC_SLOTS - n_slc, tq), F32)], axis=0)
        score = jnp.where(forced, SEL_BIG, jnp.where(eligible, imp, -SEL_BIG))
        sel = jnp.zeros(score.shape, jnp.bool_)
        for _ in range(SLC_TOPN):
            best = jnp.max(score, axis=0, keepdims=True)
            first = jnp.min(jnp.where(score == best, blk, SLC_SLOTS), axis=0, keepdims=True)
            hit = blk == first
            sel = sel | hit
            score = jnp.where(hit, -jnp.inf, score)
        mask_ref[g] = jnp.where(sel, 0.0, -MASK_BIG).astype(mask_ref.dtype)


def _select(n_q, cmp, cmp_t, tq):
    B, S, _ = n_q.shape
    G = NSA_KV_GROUPS
    ncp = cmp.shape[2]
    return pl.pallas_call(
        _select_kernel,
        grid=(B, S // tq),
        in_specs=[
            pl.BlockSpec((None, tq, N_Q), lambda b, i: (b, i, 0)),
            pl.BlockSpec((None, None, ncp, LANES), lambda b, i: (b, 0, 0, 0)),
            pl.BlockSpec((None, None, LANES, ncp), lambda b, i: (b, 1, 0, 0)),
        ],
        out_specs=[
            pl.BlockSpec((None, N_Q, tq), lambda b, i: (b, 0, i)),
            pl.BlockSpec((None, N_Q, tq), lambda b, i: (b, 0, i)),
            pl.BlockSpec((None, G, SLC_SLOTS, tq), lambda b, i: (b, 0, 0, i)),
        ],
        out_shape=[
            jax.ShapeDtypeStruct((B, N_Q, S), BF16),
            jax.ShapeDtypeStruct((B, N_Q, S), BF16),
            jax.ShapeDtypeStruct((B, G, SLC_SLOTS, S), BF16),
        ],
        scratch_shapes=[pltpu.VMEM((tq // LANES, ncp, LANES), F32)],
        compiler_params=_cparams(("parallel", "parallel")),
        name="nsa_select",
    )(n_q, cmp, cmp_t)


_TK = 256
_KAUG_W = SLC_SLOTS + N_KV
_AHEAD = 7


def _attend_kernel(qt_ref, mask_ref, ocmp_ref, gate_ref, kaug_ref, vt_ref, kw_ref, vwt_ref, y_ref,
                   qa_scr, m_scr, acc_scr, yt_scr):
    tq = qt_ref.shape[1]
    S = kaug_ref.shape[0]
    G = NSA_KV_GROUPS
    q0 = pl.program_id(1) * tq
    t_q = q0 + lax.broadcasted_iota(jnp.int32, (1, tq), 1)
    n_full = q0 // _TK
    kd = pl.multiple_of(n_full * _TK, _TK)
    causal = (kd + lax.broadcasted_iota(jnp.int32, (_TK, 1), 0)) <= t_q
    heads = [(g, hh) for g in range(G) for hh in range(NSA_HPG)]

    for g, hh in heads:
        hd = g * NSA_HPG + hh
        cols = slice(hh * tq, (hh + 1) * tq)
        qa_scr[g, 0:SLC_SLOTS, cols] = mask_ref[g]
        qa_scr[g, SLC_SLOTS:, cols] = _group_rows(qt_ref[hd * NSA_DH:(hd + 1) * NSA_DH, :], g)

    def scores(n, k0):
        g, hh = heads[n]
        return _dot(kaug_ref[pl.ds(k0, _TK), :], qa_scr[g, :, hh * tq:(hh + 1) * tq])

    def key_sweep(consume):
        def tiles(starts, diagonal):
            work = [(n, k0) for k0 in starts for n in range(len(heads))]
            pending = [scores(*w) for w in work[:_AHEAD]]
            for i, (n, k0) in enumerate(work):
                s = pending.pop(0)
                if i + _AHEAD < len(work):
                    pending.append(scores(*work[i + _AHEAD]))
                consume(*heads[n], k0, jnp.where(causal, s, -MASK_BIG) if diagonal else s)

        def pair(j, carry):
            k0 = pl.multiple_of(2 * j * _TK, _TK)
            tiles([k0, pl.multiple_of(k0 + _TK, _TK)], False)
            return carry

        lax.fori_loop(0, n_full // 2, pair, 0)

        @pl.when(n_full % 2 == 1)
        def _():
            tiles([pl.multiple_of(kd - _TK, _TK)], False)

        tiles([kd], True)

    m_scr[...] = jnp.full(m_scr.shape, NEG_BIG, F32)
    acc_scr[...] = jnp.zeros_like(acc_scr)

    def accumulate(g, hh, k0, s):
        cols = slice(hh * tq, (hh + 1) * tq)
        m_old = m_scr[g, :, cols]
        m_new = jnp.maximum(m_old, jnp.max(s, axis=0, keepdims=True))
        p = jnp.exp((s - m_new[0:1]).astype(BF16))
        acc_scr[g, :, cols] = (jnp.exp(m_old - m_new)[0:1] * acc_scr[g, :, cols]
                               + _dot(vt_ref[g, :, pl.ds(k0, _TK)], p))
        m_scr[g, :, cols] = m_new

    key_sweep(accumulate)

    gates_t = _sigmoid(gate_ref[...]).T
    wlen = min(tq + WINDOW, S)
    w0 = pl.multiple_of(jnp.maximum(q0 - WINDOW, 0), tq)
    kpos = w0 + lax.broadcasted_iota(jnp.int32, (wlen, 1), 0)
    band = (kpos <= t_q) & (kpos > t_q - WINDOW)

    def window_scores(g, hh):
        return _dot(kw_ref[pl.ds(w0, wlen), :], qa_scr[g, SLC_SLOTS:, hh * tq:(hh + 1) * tq])

    s_next = window_scores(*heads[0])
    for i, (g, hh) in enumerate(heads):
        s = jnp.where(band, s_next, NEG_BIG)
        if i + 1 < len(heads):
            s_next = window_scores(*heads[i + 1])
        hd = g * NSA_HPG + hh
        rows_h = slice(hd * NSA_DH, (hd + 1) * NSA_DH)
        e = jnp.exp(s - jnp.max(s, axis=0, keepdims=True))
        ow = _dot(vwt_ref[g, :, pl.ds(w0, wlen)], e.astype(BF16))
        o_win = ow[:NSA_DH] / ow[NSA_DH:NSA_DH + 1]
        acc = acc_scr[g, :, hh * tq:(hh + 1) * tq]
        o_slc = acc[:NSA_DH] / acc[NSA_DH:NSA_DH + 1]
        gr = GATE_COL0 + 3 * hd
        yt_scr[rows_h, :] = (gates_t[gr:gr + 1] * ocmp_ref[rows_h, :].astype(F32)
                             + gates_t[gr + 1:gr + 2] * o_slc
                             + gates_t[gr + 2:gr + 3] * o_win)
    y_ref[...] = yt_scr[...].T.astype(y_ref.dtype)


def _attend(q_t, mask_t, ocmp_t, narrow, kaug, v_t, kw, vw_t, tq):
    B, _, S = q_t.shape
    G = NSA_KV_GROUPS
    rows = NSA_HPG * tq
    return pl.pallas_call(
        _attend_kernel,
        grid=(B, S // tq),
        in_specs=[
            pl.BlockSpec((None, N_Q, tq), lambda b, i: (b, 0, i)),
            pl.BlockSpec((None, G, SLC_SLOTS, tq), lambda b, i: (b, 0, 0, i)),
            pl.BlockSpec((None, N_Q, tq), lambda b, i: (b, 0, i)),
            pl.BlockSpec((None, tq, _NARROW_W), lambda b, i: (b, i, 0)),
            pl.BlockSpec((None, S, _KAUG_W), lambda b, i: (b, 0, 0)),
            pl.BlockSpec((None, G, _VT_ROWS, S), lambda b, i: (b, 0, 0, 0)),
            pl.BlockSpec((None, S, N_KV), lambda b, i: (b, 0, 0)),
            pl.BlockSpec((None, G, _VT_ROWS, S), lambda b, i: (b, 0, 0, 0)),
        ],
        out_specs=pl.BlockSpec((None, tq, N_Q), lambda b, i: (b, i, 0)),
        out_shape=jax.ShapeDtypeStruct((B, S, N_Q), BF16),
        scratch_shapes=[
            pltpu.VMEM((G, _KAUG_W, rows), BF16),
            pltpu.VMEM((G, 8, rows), F32),
            pltpu.VMEM((G, _VT_ROWS, rows), F32),
            pltpu.VMEM((N_Q, tq), F32),
        ],
        compiler_params=_cparams(("parallel", "arbitrary")),
        name="nsa_attend",
    )(q_t, mask_t, ocmp_t, narrow, kaug, v_t, kw, vw_t)


def _merge_kernel(x_ref, ya_ref, yb_ref, ga_ref, gb_ref, wa_ref, wb_ref, wo_ref, g2_ref, x1_ref, h2_ref):
    a = _sigmoid(ga_ref[...].astype(F32)) * _dot(ya_ref[...], wa_ref[...])
    b = _sigmoid(gb_ref[...].astype(F32)) * _dot(yb_ref[...], wb_ref[...])
    x1 = x_ref[...] + _dot((a + b).astype(BF16), wo_ref[...])
    x1_ref[...] = x1
    h2_ref[...] = _rms(x1, g2_ref[...]).astype(h2_ref.dtype)


def _merge(x2d, ya, yb, ga, gb, wa, wb, wo, g2, tm):
    T = x2d.shape[0]
    row = lambda w: pl.BlockSpec((tm, w), lambda i: (i, 0))
    const = lambda r, c: pl.BlockSpec((r, c), lambda i: (0, 0))
    return pl.pallas_call(
        _merge_kernel,
        grid=(T // tm,),
        in_specs=[row(D_MODEL), row(M_V), row(N_Q), row(D_MODEL), row(D_MODEL),
                  const(M_V, D_MODEL), const(N_Q, D_MODEL), const(D_MODEL, D_MODEL), const(1, D_MODEL)],
        out_specs=[row(D_MODEL), row(D_MODEL)],
        out_shape=[jax.ShapeDtypeStruct((T, D_MODEL), F32), jax.ShapeDtypeStruct((T, D_MODEL), BF16)],
        compiler_params=_cparams(("parallel",)),
        name="merge",
    )(x2d, ya, yb, ga, gb, wa.astype(BF16), wb.astype(BF16), wo.astype(BF16), g2[None, :])


_FFN_CHUNK = 256


def _ffn_kernel(x1_ref, h2_ref, wg_ref, wu_ref, wd_ref, gf_ref, out_ref):
    h = h2_ref[...]
    acc = x1_ref[...]
    for c in range(FFN_HIDDEN // _FFN_CHUNK):
        sl = slice(c * _FFN_CHUNK, (c + 1) * _FFN_CHUNK)
        gate = _dot(h, wg_ref[:, sl])
        up = _dot(h, wu_ref[:, sl])
        acc = acc + _dot((gate * _sigmoid(gate) * up).astype(BF16), wd_ref[sl, :])
    out_ref[...] = _rms(acc, gf_ref[...])


def _ffn(x1, h2, wg, wu, wd, gf, tm):
    T = x1.shape[0]
    row = lambda: pl.BlockSpec((tm, D_MODEL), lambda i: (i, 0))
    const = lambda r, c: pl.BlockSpec((r, c), lambda i: (0, 0))
    return pl.pallas_call(
        _ffn_kernel,
        grid=(T // tm,),
        in_specs=[row(), row(), const(D_MODEL, FFN_HIDDEN), const(D_MODEL, FFN_HIDDEN),
                  const(FFN_HIDDEN, D_MODEL), const(1, D_MODEL)],
        out_specs=row(),
        out_shape=jax.ShapeDtypeStruct((T, D_MODEL), F32),
        compiler_params=_cparams(("parallel",)),
        name="ffn",
    )(x1, h2, wg.astype(BF16), wu.astype(BF16), wd.astype(BF16), gf[None, :])


def _layer(x, norm1_g, w_in, b_in, f_bias, conv_w, conv_b, mlstm_norm_g, cmp_k_pos, cmp_k_w1, cmp_k_w2,
           cmp_v_pos, cmp_v_w1, cmp_v_w2, w_branch_a, w_branch_b, w_out, norm2_g):
    B, S, D = x.shape
    T = B * S
    tm = min(256, S)

    (m_qk, m_v, m_o, n_q, gate_a, gate_b, narrow, narrow_t, cmp_kv, kaug, vs_t, k_win, vw_t) = _inproj(
        x, norm1_g, w_in, b_in, tm)
    narrow = narrow.reshape(B, S, _NARROW_W)

    q, k = _conv(m_qk.reshape(B, S, 2 * M_QK), conv_w, conv_b, min(512, S))
    y_a = _mlstm(q, k, m_v.reshape(B, S, M_V), m_o.reshape(B, S, M_V), narrow, narrow_t,
                 f_bias, mlstm_norm_g, min(256, S))

    cmp, cmp_t = _compress(cmp_kv.reshape(B, S, 2 * N_KV), jnp.stack([cmp_k_pos, cmp_v_pos]),
                           jnp.stack([cmp_k_w1, cmp_v_w1]), jnp.stack([cmp_k_w2, cmp_v_w2]))
    tq = min(256, S)
    q_t, ocmp_t, mask_t = _select(n_q.reshape(B, S, N_Q), cmp, cmp_t, tq)
    y_b = _attend(q_t, mask_t, ocmp_t, narrow, kaug.reshape(B, S, _KAUG_W), vs_t,
                  k_win.reshape(B, S, N_KV), vw_t, tq)

    return _merge(x.reshape(T, D), y_a.reshape(T, M_V), y_b.reshape(T, N_Q), gate_a, gate_b,
                  w_branch_a, w_branch_b, w_out, norm2_g, min(512, T))


def kernel(x, norm1_g, w_in, b_in, f_bias, conv_w, conv_b, mlstm_norm_g, cmp_k_pos, cmp_k_w1, cmp_k_w2,
           cmp_v_pos, cmp_v_w1, cmp_v_w2, w_branch_a, w_branch_b, w_out, norm2_g, w_ffn_gate, w_ffn_up,
           w_ffn_down, norm_f_g):
    B, S, D = x.shape
    depth = w_in.shape[0]
    assert depth == 1, "the fused final norm assumes a single layer"
    x1, h2 = _layer(x, norm1_g[0], w_in[0], b_in[0], f_bias[0], conv_w[0], conv_b[0], mlstm_norm_g[0],
                    cmp_k_pos[0], cmp_k_w1[0], cmp_k_w2[0], cmp_v_pos[0], cmp_v_w1[0], cmp_v_w2[0],
                    w_branch_a[0], w_branch_b[0], w_out[0], norm2_g[0])
    out = _ffn(x1, h2, w_ffn_gate[0], w_ffn_up[0], w_ffn_down[0], norm_f_g, min(512, B * S))
    return out.reshape(B, S, D)
```

```python
import functools

import numpy as np
import jax
import jax.numpy as jnp
from jax import lax
from jax.experimental import pallas as pl
from jax.experimental.pallas import tpu as pltpu

F32 = jnp.float32
BF16 = jnp.bfloat16

D_MODEL = 1024
MLSTM_HEADS = 4
MLSTM_DV = D_MODEL // MLSTM_HEADS
MLSTM_DQK = MLSTM_DV // 2
CONV_K = 4
NSA_DH = 64
NSA_HEADS = (D_MODEL // 2) // NSA_DH
NSA_KV_GROUPS = 2
NSA_HPG = NSA_HEADS // NSA_KV_GROUPS
CMP_BLOCK = 32
CMP_STRIDE = 16
CMP_HIDDEN = 256
SLC_BLOCK = 64
SLC_TOPN = 16
WINDOW = 512
FFN_HIDDEN = 2816
RMS_EPS = 1e-6
SEL_BIG = 1e9

M_QK = MLSTM_HEADS * MLSTM_DQK
M_V = MLSTM_HEADS * MLSTM_DV
N_Q = NSA_HEADS * NSA_DH
N_KV = NSA_KV_GROUPS * NSA_DH
IN_SPLITS = (2 * M_QK, M_V, M_V, MLSTM_HEADS, MLSTM_HEADS, N_Q, 6 * N_KV, 3 * NSA_HEADS, D_MODEL, D_MODEL)

LANES = 128
SLC_SLOTS = 128
MASK_BIG = float(2.0 ** 100)
NEG_BIG = -1e30
VMEM_LIMIT = 56 * 1024 * 1024
assert N_KV == LANES, "both KV groups are packed into one 128-lane key row"

_SEG = dict(zip(("m_qk", "m_v", "m_o", "m_i", "m_f", "n_q", "n_kv", "n_g", "gate_a", "gate_b"),
                zip(np.cumsum((0,) + IN_SPLITS[:-1]).tolist(), IN_SPLITS)))
_WIDE = ("m_qk", "m_v", "m_o", "n_q", "n_kv", "gate_a", "gate_b")
_NARROW = ("m_i", "m_f", "n_g")
_NARROW_W = LANES
GATE_COL0 = 2 * MLSTM_HEADS
_VT_ROWS = 80


def _cparams(sem):
    return pltpu.CompilerParams(dimension_semantics=sem, vmem_limit_bytes=VMEM_LIMIT)


def _sigmoid(x):
    return 1.0 / (1.0 + jnp.exp(-x))


def _log_sigmoid(x):
    return jnp.minimum(x, 0.0) - jnp.log(1.0 + jnp.exp(-jnp.abs(x)))


def _rms(x, g):
    ms = jnp.mean(x * x, axis=-1, keepdims=True)
    return x * lax.rsqrt(ms + RMS_EPS) * g


def _dot_nt(a, b):
    return lax.dot_general(a, b, (((1,), (1,)), ((), ())), preferred_element_type=F32)


def _dot(a, b):
    return jnp.dot(a, b, preferred_element_type=F32)


def _values_t(v):
    v_t = v.T
    row = lax.broadcasted_iota(jnp.int32, (_VT_ROWS - NSA_DH, v.shape[0]), 0)
    tail = jnp.where(row == 0, 1.0, 0.0)
    return [jnp.concatenate([v_t[g * NSA_DH:(g + 1) * NSA_DH], tail], axis=0).astype(BF16)
            for g in range(NSA_KV_GROUPS)]


_HALO = 8
_CONV_CHUNK = 2 * M_QK // 4
assert M_QK % _CONV_CHUNK == 0


def _inproj_kernel(x_ref, g_ref, w_ref, b_ref, cw_ref, cb_ref, q_ref, k_ref, v_ref, o_ref, nq_ref, ga_ref,
                   gb_ref, narrow_ref, narrow_t_ref, cmp_ref, kaug_ref, vst_ref, kw_ref, vwt_ref, halo_scr,
                   *, offs, tiles_per_seq):
    tm = x_ref.shape[0]
    first_of_seq = pl.program_id(0) % tiles_per_seq == 0
    h = _rms(x_ref[...], g_ref[...]).astype(BF16)

    def seg(off, width):
        return _dot(h, w_ref[:, off:off + width]) + b_ref[:, off:off + width]

    def conv_chunk(c):
        cols = slice(c * _CONV_CHUNK, (c + 1) * _CONV_CHUNK)
        qk = seg(offs["m_qk"] + c * _CONV_CHUNK, _CONV_CHUNK)
        halo = jnp.where(first_of_seq, 0.0, halo_scr[:, cols])
        halo_scr[:, cols] = qk[tm - _HALO:]
        padded = jnp.concatenate([halo, qk], axis=0)
        acc = qk * cw_ref[CONV_K - 1:CONV_K, cols] + cb_ref[:, cols]
        for j in range(1, CONV_K):
            acc = acc + pltpu.roll(padded, j, 0)[_HALO:] * cw_ref[CONV_K - 1 - j:CONV_K - j, cols]
        act = acc * _sigmoid(acc)
        if c * _CONV_CHUNK < M_QK:
            q_ref[:, cols] = act.astype(BF16)
        else:
            k_ref[:, c * _CONV_CHUNK - M_QK:(c + 1) * _CONV_CHUNK - M_QK] = (act * (MLSTM_DQK ** -0.5)).astype(BF16)

    others = [(v_ref, offs["m_v"], M_V), (o_ref, offs["m_o"], M_V), (ga_ref, offs["gate_a"], D_MODEL),
              (gb_ref, offs["gate_b"], D_MODEL)]
    for c, (ref, off, width) in enumerate(others):
        conv_chunk(c)
        ref[...] = seg(off, width).astype(BF16)
    nq_ref[...] = seg(offs["n_q"], N_Q).astype(BF16)
    narrow = seg(offs["narrow"], _NARROW_W)
    narrow_ref[...] = narrow
    narrow_t_ref[...] = narrow.T

    kv0 = offs["n_kv"]
    cmp_ref[...] = seg(kv0, 2 * N_KV)
    pos = (pl.program_id(0) % tiles_per_seq) * tm + lax.broadcasted_iota(jnp.int32, (tm, SLC_SLOTS), 0)
    onehot = (pos // SLC_BLOCK == lax.broadcasted_iota(jnp.int32, (tm, SLC_SLOTS), 1)).astype(BF16)
    kaug_ref[...] = jnp.concatenate([onehot, seg(kv0 + 2 * N_KV, N_KV).astype(BF16)], axis=1)
    for g, vt in enumerate(_values_t(seg(kv0 + 3 * N_KV, N_KV))):
        vst_ref[g] = vt
    kw_ref[...] = seg(kv0 + 4 * N_KV, N_KV).astype(BF16)
    for g, vt in enumerate(_values_t(seg(kv0 + 5 * N_KV, N_KV))):
        vwt_ref[g] = vt


def _inproj(x, norm_g, w_in, b_in, conv_w, conv_b, tm):
    B, S, _ = x.shape
    T = B * S
    G = NSA_KV_GROUPS
    cols = [w_in[:, _SEG[n][0]:_SEG[n][0] + _SEG[n][1]] for n in _WIDE]
    bias = [b_in[_SEG[n][0]:_SEG[n][0] + _SEG[n][1]] for n in _WIDE]
    narrow_w = jnp.concatenate([w_in[:, _SEG[n][0]:_SEG[n][0] + _SEG[n][1]] for n in _NARROW], axis=1)
    narrow_b = jnp.concatenate([b_in[_SEG[n][0]:_SEG[n][0] + _SEG[n][1]] for n in _NARROW])
    pad = _NARROW_W - narrow_w.shape[1]
    cols.append(jnp.pad(narrow_w, ((0, 0), (0, pad))))
    bias.append(jnp.pad(narrow_b, (0, pad)))
    w = jnp.concatenate(cols, axis=1).astype(BF16)
    b = jnp.concatenate(bias)[None, :].astype(F32)
    widths = [_SEG[n][1] for n in _WIDE] + [_NARROW_W]
    offs = dict(zip(_WIDE + ("narrow",), np.cumsum([0] + widths[:-1]).tolist()))
    n_tot = w.shape[1]
    tps = S // tm

    def rows(width, dtype):
        return pl.BlockSpec((tm, width), lambda i: (i, 0)), jax.ShapeDtypeStruct((T, width), dtype)

    def rows_t(lead, dtype):
        nd = len(lead)
        return (pl.BlockSpec((None,) + lead + (tm,), lambda i: (i // tps,) + (0,) * nd + (i % tps,)),
                jax.ShapeDtypeStruct((B,) + lead + (S,), dtype))

    outs = [rows(M_QK, BF16), rows(M_QK, BF16), rows(M_V, BF16), rows(M_V, BF16), rows(N_Q, BF16),
            rows(D_MODEL, BF16), rows(D_MODEL, BF16), rows(_NARROW_W, F32), rows_t((_NARROW_W,), F32),
            rows(2 * N_KV, F32), rows(SLC_SLOTS + N_KV, BF16), rows_t((G, _VT_ROWS), BF16), rows(N_KV, BF16),
            rows_t((G, _VT_ROWS), BF16)]
    return pl.pallas_call(
        functools.partial(_inproj_kernel, offs=offs, tiles_per_seq=tps),
        grid=(T // tm,),
        in_specs=[
            pl.BlockSpec((tm, D_MODEL), lambda i: (i, 0)),
            pl.BlockSpec((1, D_MODEL), lambda i: (0, 0)),
            pl.BlockSpec((D_MODEL, n_tot), lambda i: (0, 0)),
            pl.BlockSpec((1, n_tot), lambda i: (0, 0)),
            pl.BlockSpec((CONV_K, 2 * M_QK), lambda i: (0, 0)),
            pl.BlockSpec((1, 2 * M_QK), lambda i: (0, 0)),
        ],
        out_specs=[o[0] for o in outs],
        out_shape=[o[1] for o in outs],
        scratch_shapes=[pltpu.VMEM((_HALO, 2 * M_QK), F32)],
        compiler_params=_cparams(("arbitrary",)),
        name="inproj",
    )(x.reshape(T, D_MODEL), norm_g[None, :], w, b, conv_w, conv_b[None, :])


def _mlstm_kernel(q_ref, k_ref, v_ref, o_ref, gr_ref, gc_ref, fb_ref, ng_ref, y_ref, c_scr, m_scr):
    H, DK, DV = MLSTM_HEADS, MLSTM_DQK, MLSTM_DV
    L = q_ref.shape[0]
    heads = range(H)

    @pl.when(pl.program_id(1) == 0)
    def _():
        c_scr[...] = jnp.zeros_like(c_scr)
        m_scr[...] = jnp.zeros_like(m_scr)

    r_t = lax.broadcasted_iota(jnp.int32, (L, L), 0)
    r_s = lax.broadcasted_iota(jnp.int32, (L, L), 1)
    causal = r_s <= r_t
    upper = r_t <= r_s
    gates = gc_ref[...]

    qs = [q_ref[:, h * DK:(h + 1) * DK] for h in heads]
    ks = [k_ref[:, h * DK:(h + 1) * DK] for h in heads]
    qk = [_dot_nt(qs[h], ks[h]) for h in heads]
    v_ext = [jnp.concatenate([v_ref[:, h * DV:(h + 1) * DV], jnp.ones((L, LANES), BF16)], axis=1) for h in heads]
    state = [c_scr[h] for h in heads]
    inter = [_dot(qs[h], state[h].astype(BF16)) for h in heads]

    stab = []
    for h in heads:
        fb = fb_ref[h:h + 1, 0:1]
        i_row = gr_ref[h:h + 1, :]
        lf_row = _log_sigmoid(gr_ref[H + h:H + h + 1, :] + fb)
        i_col = gates[:, h:h + 1]
        lf_col = _log_sigmoid(gates[:, H + h:H + h + 1] + fb)
        m_prev = m_scr[h]
        b_col = jnp.sum(jnp.where(causal, lf_row, 0.0), axis=1, keepdims=True)
        b_row = jnp.sum(jnp.where(upper, lf_col, 0.0), axis=0, keepdims=True)
        b_last = jnp.sum(lf_row, axis=1, keepdims=True)
        d_intra = jnp.where(causal, b_col - b_row + i_row, NEG_BIG)
        d_inter = b_col + m_prev
        m_t = jnp.maximum(d_inter, jnp.max(d_intra, axis=1, keepdims=True))
        d_state_row = b_last - b_row + i_row
        m_new = jnp.maximum(b_last + m_prev, jnp.max(d_state_row, axis=1, keepdims=True))
        stab.append(dict(w_intra=jnp.exp(d_intra - m_t), w_inter=jnp.exp(d_inter - m_t), floor=jnp.exp(-m_t),
                         w_state=jnp.exp(b_last - b_col + i_col - m_new),
                         decay=jnp.exp(b_last + m_prev - m_new), m_new=m_new))

    s = [(qk[h] * stab[h]["w_intra"]).astype(BF16) for h in heads]
    num = [_dot(s[h], v_ext[h]) + stab[h]["w_inter"] * inter[h] for h in heads]
    for h in heads:
        hid = num[h][:, :DV] / jnp.maximum(jnp.abs(num[h][:, DV:DV + 1]), stab[h]["floor"])
        hid = _rms(hid, ng_ref[:, h * DV:(h + 1) * DV])
        y_ref[:, h * DV:(h + 1) * DV] = (_sigmoid(o_ref[:, h * DV:(h + 1) * DV].astype(F32)) * hid).astype(y_ref.dtype)
    k_t = [ks[h].astype(F32).T.astype(BF16) for h in heads]
    vw = [(v_ext[h].astype(F32) * stab[h]["w_state"]).astype(BF16) for h in heads]
    for h in heads:
        c_scr[h] = stab[h]["decay"] * state[h] + _dot(k_t[h], vw[h])
        m_scr[h] = stab[h]["m_new"]


def _mlstm(q, k, v, o, narrow, narrow_t, f_bias, norm_g, L):
    B, S, _ = q.shape
    H, DK, DV = MLSTM_HEADS, MLSTM_DQK, MLSTM_DV
    fb = jnp.broadcast_to(f_bias.astype(F32)[:, None], (H, LANES))
    return pl.pallas_call(
        _mlstm_kernel,
        grid=(B, S // L),
        in_specs=[
            pl.BlockSpec((None, L, M_QK), lambda b, c: (b, c, 0)),
            pl.BlockSpec((None, L, M_QK), lambda b, c: (b, c, 0)),
            pl.BlockSpec((None, L, M_V), lambda b, c: (b, c, 0)),
            pl.BlockSpec((None, L, M_V), lambda b, c: (b, c, 0)),
            pl.BlockSpec((None, 2 * H, L), lambda b, c: (b, 0, c)),
            pl.BlockSpec((None, L, _NARROW_W), lambda b, c: (b, c, 0)),
            pl.BlockSpec((H, LANES), lambda b, c: (0, 0)),
            pl.BlockSpec((1, M_V), lambda b, c: (0, 0)),
        ],
        out_specs=pl.BlockSpec((None, L, M_V), lambda b, c: (b, c, 0)),
        out_shape=jax.ShapeDtypeStruct((B, S, M_V), BF16),
        scratch_shapes=[pltpu.VMEM((H, DK, DV + LANES), F32), pltpu.VMEM((H, 1, 1), F32)],
        compiler_params=_cparams(("parallel", "arbitrary")),
        name="mlstm",
    )(q, k, v, o, narrow_t, narrow, fb, norm_g[None, :])


def _compress_kernel(x_ref, pos_ref, w1_ref, w2_ref, out_ref, out_t_ref, x_scr):
    S = x_ref.shape[0]
    n = S // CMP_STRIDE
    x_scr[0:S] = x_ref[...]
    x_scr[S:] = jnp.zeros((CMP_STRIDE, LANES), F32)
    acc = jnp.zeros((n, w1_ref.shape[2]), F32)
    for l in range(CMP_BLOCK):
        tok = x_scr[pl.ds(l, n, stride=CMP_STRIDE), :] + pos_ref[l:l + 1, :]
        acc = acc + _dot(tok.astype(BF16), w1_ref[l])
    hid = acc * _sigmoid(acc)
    out = _dot(hid.astype(BF16), w2_ref[...])
    row = lax.broadcasted_iota(jnp.int32, out.shape, 0)
    out = jnp.where(row < n - 1, out, 0.0)
    out_ref[...] = out.astype(out_ref.dtype)
    out_t_ref[...] = out.T.astype(out_t_ref.dtype)


def _block_diag2(w):
    z = jnp.zeros_like(w)
    return jnp.concatenate([jnp.concatenate([w, z], axis=-1), jnp.concatenate([z, w], axis=-1)], axis=-2)


def _compress(cmp_kv, pos, w1, w2):
    B, S, _ = cmp_kv.shape
    n = S // CMP_STRIDE
    pos2 = jnp.concatenate([pos, pos], axis=-1)
    w1b = _block_diag2(w1.reshape(2, CMP_BLOCK, NSA_DH, CMP_HIDDEN)).astype(BF16)
    w2b = _block_diag2(w2).astype(BF16)
    return pl.pallas_call(
        _compress_kernel,
        grid=(B, 2),
        in_specs=[
            pl.BlockSpec((None, S, LANES), lambda b, a: (b, 0, a)),
            pl.BlockSpec((None, CMP_BLOCK, LANES), lambda b, a: (a, 0, 0)),
            pl.BlockSpec((None, CMP_BLOCK, LANES, 2 * CMP_HIDDEN), lambda b, a: (a, 0, 0, 0)),
            pl.BlockSpec((None, 2 * CMP_HIDDEN, LANES), lambda b, a: (a, 0, 0)),
        ],
        out_specs=[pl.BlockSpec((None, None, n, LANES), lambda b, a: (b, a, 0, 0)),
                   pl.BlockSpec((None, None, LANES, n), lambda b, a: (b, a, 0, 0))],
        out_shape=[jax.ShapeDtypeStruct((B, 2, n, LANES), BF16),
                   jax.ShapeDtypeStruct((B, 2, LANES, n), BF16)],
        scratch_shapes=[pltpu.VMEM((S + CMP_STRIDE, LANES), F32)],
        compiler_params=_cparams(("parallel", "parallel")),
        name="nsa_compress",
    )(cmp_kv, pos2, w1b, w2b)


_CMP_PER_SLC = SLC_BLOCK // CMP_STRIDE
_FORCED = 3


def _group_rows(q_h, g):
    z = jnp.zeros_like(q_h)
    return jnp.concatenate([q_h, z] if g == 0 else [z, q_h], axis=0)


def _select_kernel(q_ref, kc_ref, vct_ref, qt_ref, ocmp_ref, mask_ref, ps_scr):
    tq = q_ref.shape[0]
    t0 = pl.program_id(1) * tq
    q_t = (q_ref[...].astype(F32) * (NSA_DH ** -0.5)).T.astype(BF16)
    qt_ref[...] = q_t
    need = (t0 + tq) // CMP_STRIDE
    for nk in range(LANES, kc_ref.shape[0] + 1, LANES):
        pl.when((need - 1) // LANES == nk // LANES - 1)(
            functools.partial(_select_body, q_t, kc_ref, vct_ref, ocmp_ref, mask_ref, ps_scr, t0, nk))


def _select_body(q_t, kc_ref, vct_ref, ocmp_ref, mask_ref, ps_scr, t0, ncp):
    tq = q_t.shape[1]
    n_slc = ncp // _CMP_PER_SLC
    t_row = t0 + lax.broadcasted_iota(jnp.int32, (1, tq), 1)
    c_col = lax.broadcasted_iota(jnp.int32, (ncp, 1), 0)
    visible = (c_col * CMP_STRIDE + (CMP_BLOCK - 1)) <= t_row
    any_visible = t_row >= CMP_BLOCK - 1

    blk = lax.broadcasted_iota(jnp.int32, (SLC_SLOTS, tq), 0)
    cur = (t0 + lax.broadcasted_iota(jnp.int32, (SLC_SLOTS, tq), 1)) // SLC_BLOCK
    forced = (blk == 0) | (blk == cur) | (blk == cur - 1)
    eligible = blk <= cur

    def scores(hd):
        return _dot(kc_ref[0:ncp, :], _group_rows(q_t[hd * NSA_DH:(hd + 1) * NSA_DH], hd // NSA_HPG))

    s_next = scores(0)
    for g in range(NSA_KV_GROUPS):
        vct = vct_ref[g * NSA_DH:(g + 1) * NSA_DH, 0:ncp]
        for hh in range(NSA_HPG):
            hd = g * NSA_HPG + hh
            s = jnp.where(visible, s_next, NEG_BIG)
            if hd + 1 < NSA_HEADS:
                s_next = scores(hd + 1)
            e = jnp.exp(s - jnp.max(s, axis=0, keepdims=True))
            inv = jnp.where(any_visible, 1.0 / jnp.sum(e, axis=0, keepdims=True), 0.0)
            p = e * inv
            ocmp_ref[hd * NSA_DH:(hd + 1) * NSA_DH, :] = _dot(vct, p.astype(BF16)).astype(ocmp_ref.dtype)
            for c in range(tq // LANES):
                if hh == 0:
                    ps_scr[c, 0:ncp, :] = p[:, c * LANES:(c + 1) * LANES]
                else:
                    ps_scr[c, 0:ncp, :] += p[:, c * LANES:(c + 1) * LANES]
        lanes = [jnp.concatenate([ps_scr[c, pl.ds(r, n_slc, stride=_CMP_PER_SLC), :]
                                  for c in range(tq // LANES)], axis=1) for r in range(_CMP_PER_SLC)]
        row = lax.broadcasted_iota(jnp.int32, (n_slc, tq), 0)
        before = jnp.where(row == 0, 0.0, pltpu.roll(lanes[3], 1, 0))
        imp = before + 2.0 * (lanes[0] + lanes[1] + lanes[2]) + lanes[3]
        if n_slc < SLC_SLOTS:
            imp = jnp.concatenate([imp, jnp.zeros((SLC_SLOTS - n_slc, tq), F32)], axis=0)
        score = jnp.where(forced, -jnp.inf, jnp.where(eligible, imp, -SEL_BIG))
        for _ in range(SLC_TOPN - _FORCED):
            best = jnp.max(score, axis=0, keepdims=True)
            first = jnp.min(jnp.where(score == best, blk, SLC_SLOTS), axis=0, keepdims=True)
            score = jnp.where(blk == first, -jnp.inf, score)
        mask_ref[g] = jnp.where(score == -jnp.inf, 0.0, -MASK_BIG).astype(mask_ref.dtype)


def _select(n_q, cmp, cmp_t, tq):
    B, S, _ = n_q.shape
    G = NSA_KV_GROUPS
    ncp = cmp.shape[2]
    return pl.pallas_call(
        _select_kernel,
        grid=(B, S // tq),
        in_specs=[
            pl.BlockSpec((None, tq, N_Q), lambda b, i: (b, i, 0)),
            pl.BlockSpec((None, None, ncp, LANES), lambda b, i: (b, 0, 0, 0)),
            pl.BlockSpec((None, None, LANES, ncp), lambda b, i: (b, 1, 0, 0)),
        ],
        out_specs=[
            pl.BlockSpec((None, N_Q, tq), lambda b, i: (b, 0, i)),
            pl.BlockSpec((None, N_Q, tq), lambda b, i: (b, 0, i)),
            pl.BlockSpec((None, G, SLC_SLOTS, tq), lambda b, i: (b, 0, 0, i)),
        ],
        out_shape=[
            jax.ShapeDtypeStruct((B, N_Q, S), BF16),
            jax.ShapeDtypeStruct((B, N_Q, S), BF16),
            jax.ShapeDtypeStruct((B, G, SLC_SLOTS, S), BF16),
        ],
        scratch_shapes=[pltpu.VMEM((tq // LANES, ncp, LANES), F32)],
        compiler_params=_cparams(("parallel", "parallel")),
        name="nsa_select",
    )(n_q, cmp, cmp_t)


_TK = 256
_KAUG_W = SLC_SLOTS + N_KV
_AHEAD = 7
_TILES_PER_TRIP = 4


def _pipelined(work, score, consume):
    pending = [score(item) for item in work[:_AHEAD]]
    for i, item in enumerate(work):
        s = pending.pop(0)
        if i + _AHEAD < len(work):
            pending.append(score(work[i + _AHEAD]))
        consume(item, s)


def _attend_kernel(qt_ref, mask_ref, ocmp_ref, gate_ref, kaug_ref, vt_ref, kw_ref, vwt_ref, y_ref,
                   qa_scr, m_scr, acc_scr, yt_scr):
    tq = qt_ref.shape[1]
    S = kaug_ref.shape[0]
    G = NSA_KV_GROUPS
    q0 = pl.program_id(1) * tq
    t_q = q0 + lax.broadcasted_iota(jnp.int32, (1, tq), 1)
    n_full = q0 // _TK
    kd = pl.multiple_of(n_full * _TK, _TK)
    causal = (kd + lax.broadcasted_iota(jnp.int32, (_TK, 1), 0)) <= t_q
    heads = [(g, hh) for g in range(G) for hh in range(NSA_HPG)]

    for g, hh in heads:
        hd = g * NSA_HPG + hh
        cols = slice(hh * tq, (hh + 1) * tq)
        qa_scr[g, 0:SLC_SLOTS, cols] = mask_ref[g]
        qa_scr[g, SLC_SLOTS:, cols] = _group_rows(qt_ref[hd * NSA_DH:(hd + 1) * NSA_DH, :], g)

    m_scr[...] = jnp.full(m_scr.shape, NEG_BIG, F32)
    acc_scr[...] = jnp.zeros_like(acc_scr)

    def selected_scores(item):
        (g, hh), k0 = item
        return _dot(kaug_ref[pl.ds(k0, _TK), :], qa_scr[g, :, hh * tq:(hh + 1) * tq])

    def selected_tiles(starts, diagonal):
        def consume(item, s):
            (g, hh), k0 = item
            cols = slice(hh * tq, (hh + 1) * tq)
            if diagonal:
                s = jnp.where(causal, s, -MASK_BIG)
            m_old = m_scr[g, :, cols]
            m_new = jnp.maximum(m_old, jnp.max(s, axis=0, keepdims=True).astype(BF16).astype(F32))
            p = jnp.exp(s.astype(BF16) - m_new[0:1].astype(BF16))
            acc_scr[g, :, cols] = (jnp.exp(m_old - m_new)[0:1] * acc_scr[g, :, cols]
                                   + _dot(vt_ref[g, :, pl.ds(k0, _TK)], p))
            m_scr[g, :, cols] = m_new
        _pipelined([(head, k0) for k0 in starts for head in heads], selected_scores, consume)

    def run_tiles(first, count):
        selected_tiles([pl.multiple_of(first + n * _TK, _TK) for n in range(count)], False)

    def stretch(j, carry):
        run_tiles(j * (_TILES_PER_TRIP * _TK), _TILES_PER_TRIP)
        return carry

    lax.fori_loop(0, n_full // _TILES_PER_TRIP, stretch, 0)
    count = _TILES_PER_TRIP // 2
    while count:
        pl.when(n_full & count != 0)(functools.partial(
            run_tiles, (n_full - n_full % (2 * count)) * _TK, count))
        count //= 2
    selected_tiles([kd], True)

    gates_t = _sigmoid(gate_ref[...]).T
    wlen = min(tq + WINDOW, S)
    w0 = pl.multiple_of(jnp.maximum(q0 - WINDOW, 0), tq)

    n_wt = wlen // _TK

    def window_scores(g, hh):
        q_cols = qa_scr[g, SLC_SLOTS:, hh * tq:(hh + 1) * tq]
        return [_dot(kw_ref[pl.ds(w0 + j * _TK, _TK), :], q_cols) for j in range(n_wt)]

    def window(band_mask):
        s_next = window_scores(*heads[0])
        for i, (g, hh) in enumerate(heads):
            s = band_mask(jnp.concatenate(s_next, axis=0))
            if i + 1 < len(heads):
                s_next = window_scores(*heads[i + 1])
            hd = g * NSA_HPG + hh
            rows_h = slice(hd * NSA_DH, (hd + 1) * NSA_DH)
            e = jnp.exp(s.astype(BF16) - jnp.max(s, axis=0, keepdims=True).astype(BF16))
            ow = sum(_dot(vwt_ref[g, :, pl.ds(w0 + j * _TK, _TK)], e[j * _TK:(j + 1) * _TK])
                     for j in range(n_wt))
            acc = acc_scr[g, :, hh * tq:(hh + 1) * tq]
            gr = GATE_COL0 + 3 * hd
            yt_scr[rows_h, :] = (gates_t[gr:gr + 1] * ocmp_ref[rows_h, :].astype(F32)
                                 + gates_t[gr + 1:gr + 2] * (acc[:NSA_DH] / acc[NSA_DH:NSA_DH + 1])
                                 + gates_t[gr + 2:gr + 3] * (ow[:NSA_DH] / ow[NSA_DH:NSA_DH + 1]))

    def general_band(s):
        kpos = w0 + lax.broadcasted_iota(jnp.int32, (wlen, 1), 0)
        return jnp.where((kpos <= t_q) & (kpos > t_q - WINDOW), s, NEG_BIG)

    if wlen == tq + WINDOW and WINDOW % tq == 0:
        r = lax.broadcasted_iota(jnp.int32, (tq, tq), 0)
        c = lax.broadcasted_iota(jnp.int32, (tq, tq), 1)

        def interior_band(s):
            return jnp.concatenate([jnp.where(r > c, s[:tq], NEG_BIG), s[tq:WINDOW],
                                    jnp.where(r <= c, s[WINDOW:], NEG_BIG)], axis=0)

        pl.when(q0 >= WINDOW)(functools.partial(window, interior_band))
        pl.when(q0 < WINDOW)(functools.partial(window, general_band))
    else:
        window(general_band)
    y_ref[...] = yt_scr[...].T.astype(y_ref.dtype)


def _attend(q_t, mask_t, ocmp_t, narrow, kaug, v_t, kw, vw_t, tq):
    B, _, S = q_t.shape
    G = NSA_KV_GROUPS
    rows = NSA_HPG * tq
    return pl.pallas_call(
        _attend_kernel,
        grid=(B, S // tq),
        in_specs=[
            pl.BlockSpec((None, N_Q, tq), lambda b, i: (b, 0, i)),
            pl.BlockSpec((None, G, SLC_SLOTS, tq), lambda b, i: (b, 0, 0, i)),
            pl.BlockSpec((None, N_Q, tq), lambda b, i: (b, 0, i)),
            pl.BlockSpec((None, tq, _NARROW_W), lambda b, i: (b, i, 0)),
            pl.BlockSpec((None, S, _KAUG_W), lambda b, i: (b, 0, 0)),
            pl.BlockSpec((None, G, _VT_ROWS, S), lambda b, i: (b, 0, 0, 0)),
            pl.BlockSpec((None, S, N_KV), lambda b, i: (b, 0, 0)),
            pl.BlockSpec((None, G, _VT_ROWS, S), lambda b, i: (b, 0, 0, 0)),
        ],
        out_specs=pl.BlockSpec((None, tq, N_Q), lambda b, i: (b, i, 0)),
        out_shape=jax.ShapeDtypeStruct((B, S, N_Q), BF16),
        scratch_shapes=[
            pltpu.VMEM((G, _KAUG_W, rows), BF16),
            pltpu.VMEM((G, 8, rows), F32),
            pltpu.VMEM((G, _VT_ROWS, rows), F32),
            pltpu.VMEM((N_Q, tq), F32),
        ],
        compiler_params=_cparams(("parallel", "arbitrary")),
        name="nsa_attend",
    )(q_t, mask_t, ocmp_t, narrow, kaug, v_t, kw, vw_t)


def _merge_kernel(x_ref, ya_ref, yb_ref, ga_ref, gb_ref, wa_ref, wb_ref, wo_ref, g2_ref, x1_ref, h2_ref):
    a = _sigmoid(ga_ref[...].astype(F32)) * _dot(ya_ref[...], wa_ref[...])
    b = _sigmoid(gb_ref[...].astype(F32)) * _dot(yb_ref[...], wb_ref[...])
    x1 = x_ref[...] + _dot((a + b).astype(BF16), wo_ref[...])
    x1_ref[...] = x1
    h2_ref[...] = _rms(x1, g2_ref[...]).astype(h2_ref.dtype)


def _merge(x2d, ya, yb, ga, gb, wa, wb, wo, g2, tm):
    T = x2d.shape[0]
    row = lambda w: pl.BlockSpec((tm, w), lambda i: (i, 0))
    const = lambda r, c: pl.BlockSpec((r, c), lambda i: (0, 0))
    return pl.pallas_call(
        _merge_kernel,
        grid=(T // tm,),
        in_specs=[row(D_MODEL), row(M_V), row(N_Q), row(D_MODEL), row(D_MODEL),
                  const(M_V, D_MODEL), const(N_Q, D_MODEL), const(D_MODEL, D_MODEL), const(1, D_MODEL)],
        out_specs=[row(D_MODEL), row(D_MODEL)],
        out_shape=[jax.ShapeDtypeStruct((T, D_MODEL), F32), jax.ShapeDtypeStruct((T, D_MODEL), BF16)],
        compiler_params=_cparams(("parallel",)),
        name="merge",
    )(x2d, ya, yb, ga, gb, wa.astype(BF16), wb.astype(BF16), wo.astype(BF16), g2[None, :])


_FFN_CHUNK = 256


def _ffn_kernel(x1_ref, h2_ref, wg_ref, wu_ref, wd_ref, gf_ref, out_ref):
    h = h2_ref[...]
    acc = x1_ref[...]
    for c in range(FFN_HIDDEN // _FFN_CHUNK):
        sl = slice(c * _FFN_CHUNK, (c + 1) * _FFN_CHUNK)
        gate = _dot(h, wg_ref[:, sl])
        up = _dot(h, wu_ref[:, sl])
        acc = acc + _dot((gate * _sigmoid(gate) * up).astype(BF16), wd_ref[sl, :])
    out_ref[...] = _rms(acc, gf_ref[...])


def _ffn(x1, h2, wg, wu, wd, gf, tm):
    T = x1.shape[0]
    row = lambda: pl.BlockSpec((tm, D_MODEL), lambda i: (i, 0))
    const = lambda r, c: pl.BlockSpec((r, c), lambda i: (0, 0))
    return pl.pallas_call(
        _ffn_kernel,
        grid=(T // tm,),
        in_specs=[row(), row(), const(D_MODEL, FFN_HIDDEN), const(D_MODEL, FFN_HIDDEN),
                  const(FFN_HIDDEN, D_MODEL), const(1, D_MODEL)],
        out_specs=row(),
        out_shape=jax.ShapeDtypeStruct((T, D_MODEL), F32),
        compiler_params=_cparams(("parallel",)),
        name="ffn",
    )(x1, h2, wg.astype(BF16), wu.astype(BF16), wd.astype(BF16), gf[None, :])


def _layer(x, norm1_g, w_in, b_in, f_bias, conv_w, conv_b, mlstm_norm_g, cmp_k_pos, cmp_k_w1, cmp_k_w2,
           cmp_v_pos, cmp_v_w1, cmp_v_w2, w_branch_a, w_branch_b, w_out, norm2_g):
    B, S, D = x.shape
    T = B * S
    tm = min(256, S)

    (m_q, m_k, m_v, m_o, n_q, gate_a, gate_b, narrow, narrow_t, cmp_kv, kaug, vs_t, k_win, vw_t) = _inproj(
        x, norm1_g, w_in, b_in, conv_w, conv_b, tm)
    narrow = narrow.reshape(B, S, _NARROW_W)

    y_a = _mlstm(m_q.reshape(B, S, M_QK), m_k.reshape(B, S, M_QK), m_v.reshape(B, S, M_V),
                 m_o.reshape(B, S, M_V), narrow, narrow_t, f_bias, mlstm_norm_g, min(256, S))

    cmp, cmp_t = _compress(cmp_kv.reshape(B, S, 2 * N_KV), jnp.stack([cmp_k_pos, cmp_v_pos]),
                           jnp.stack([cmp_k_w1, cmp_v_w1]), jnp.stack([cmp_k_w2, cmp_v_w2]))
    tq = min(256, S)
    q_t, ocmp_t, mask_t = _select(n_q.reshape(B, S, N_Q), cmp, cmp_t, tq)
    y_b = _attend(q_t, mask_t, ocmp_t, narrow, kaug.reshape(B, S, _KAUG_W), vs_t,
                  k_win.reshape(B, S, N_KV), vw_t, tq)

    return _merge(x.reshape(T, D), y_a.reshape(T, M_V), y_b.reshape(T, N_Q), gate_a, gate_b,
                  w_branch_a, w_branch_b, w_out, norm2_g, min(512, T))


def kernel(x, norm1_g, w_in, b_in, f_bias, conv_w, conv_b, mlstm_norm_g, cmp_k_pos, cmp_k_w1, cmp_k_w2,
           cmp_v_pos, cmp_v_w1, cmp_v_w2, w_branch_a, w_branch_b, w_out, norm2_g, w_ffn_gate, w_ffn_up,
           w_ffn_down, norm_f_g):
    B, S, D = x.shape
    depth = w_in.shape[0]
    assert depth == 1, "the fused final norm assumes a single layer"
    x1, h2 = _layer(x, norm1_g[0], w_in[0], b_in[0], f_bias[0], conv_w[0], conv_b[0], mlstm_norm_g[0],
                    cmp_k_pos[0], cmp_k_w1[0], cmp_k_w2[0], cmp_v_pos[0], cmp_v_w1[0], cmp_v_w2[0],
                    w_branch_a[0], w_branch_b[0], w_out[0], norm2_g[0])
    out = _ffn(x1, h2, w_ffn_gate[0], w_ffn_up[0], w_ffn_down[0], norm_f_g, min(512, B * S))
    return out.reshape(B, S, D)
```

```python
import functools

import numpy as np
import jax
import jax.numpy as jnp
from jax import lax
from jax.experimental import pallas as pl
from jax.experimental.pallas import tpu as pltpu

F32 = jnp.float32
BF16 = jnp.bfloat16

D_MODEL = 1024
MLSTM_HEADS = 4
MLSTM_DV = D_MODEL // MLSTM_HEADS
MLSTM_DQK = MLSTM_DV // 2
CONV_K = 4
NSA_DH = 64
NSA_HEADS = (D_MODEL // 2) // NSA_DH
NSA_KV_GROUPS = 2
NSA_HPG = NSA_HEADS // NSA_KV_GROUPS
CMP_BLOCK = 32
CMP_STRIDE = 16
CMP_HIDDEN = 256
SLC_BLOCK = 64
SLC_TOPN = 16
WINDOW = 512
FFN_HIDDEN = 2816
RMS_EPS = 1e-6
SEL_BIG = 1e9

M_QK = MLSTM_HEADS * MLSTM_DQK
M_V = MLSTM_HEADS * MLSTM_DV
N_Q = NSA_HEADS * NSA_DH
N_KV = NSA_KV_GROUPS * NSA_DH
IN_SPLITS = (2 * M_QK, M_V, M_V, MLSTM_HEADS, MLSTM_HEADS, N_Q, 6 * N_KV, 3 * NSA_HEADS, D_MODEL, D_MODEL)

LANES = 128
SLC_SLOTS = 128
MASK_BIG = float(2.0 ** 100)
NEG_BIG = -1e30
VMEM_LIMIT = 56 * 1024 * 1024
assert N_KV == LANES, "both KV groups are packed into one 128-lane key row"

_SEG = dict(zip(("m_qk", "m_v", "m_o", "m_i", "m_f", "n_q", "n_kv", "n_g", "gate_a", "gate_b"),
                zip(np.cumsum((0,) + IN_SPLITS[:-1]).tolist(), IN_SPLITS)))
_WIDE = ("m_qk", "m_v", "m_o", "n_q", "n_kv", "gate_a", "gate_b")
_NARROW = ("m_i", "m_f", "n_g")
_NARROW_W = LANES
GATE_COL0 = 2 * MLSTM_HEADS
_VT_ROWS = 80


def _cparams(sem):
    return pltpu.CompilerParams(dimension_semantics=sem, vmem_limit_bytes=VMEM_LIMIT)


def _sigmoid(x):
    return 1.0 / (1.0 + jnp.exp(-x))


def _log_sigmoid(x):
    return jnp.minimum(x, 0.0) - jnp.log(1.0 + jnp.exp(-jnp.abs(x)))


def _rms(x, g):
    ms = jnp.mean(x * x, axis=-1, keepdims=True)
    return x * lax.rsqrt(ms + RMS_EPS) * g


def _dot_nt(a, b):
    return lax.dot_general(a, b, (((1,), (1,)), ((), ())), preferred_element_type=F32)


def _dot(a, b):
    return jnp.dot(a, b, preferred_element_type=F32)


def _values_t(v):
    v_t = v.T
    row = lax.broadcasted_iota(jnp.int32, (_VT_ROWS - NSA_DH, v.shape[0]), 0)
    tail = jnp.where(row == 0, 1.0, 0.0)
    return [jnp.concatenate([v_t[g * NSA_DH:(g + 1) * NSA_DH], tail], axis=0).astype(BF16)
            for g in range(NSA_KV_GROUPS)]


_HALO = 8
_CONV_CHUNK = 2 * M_QK // 4
assert M_QK % _CONV_CHUNK == 0


def _inproj_kernel(x_ref, g_ref, w_ref, b_ref, cw_ref, cb_ref, q_ref, k_ref, v_ref, o_ref, nq_ref, ga_ref,
                   gb_ref, narrow_ref, narrow_t_ref, cmp_ref, kaug_ref, vst_ref, kw_ref, vwt_ref, halo_scr,
                   *, offs, tiles_per_seq):
    tm = x_ref.shape[0]
    first_of_seq = pl.program_id(0) % tiles_per_seq == 0
    h = _rms(x_ref[...], g_ref[...]).astype(BF16)

    def seg(off, width):
        return _dot(h, w_ref[:, off:off + width]) + b_ref[:, off:off + width]

    def conv_chunk(c):
        cols = slice(c * _CONV_CHUNK, (c + 1) * _CONV_CHUNK)
        qk = seg(offs["m_qk"] + c * _CONV_CHUNK, _CONV_CHUNK)
        halo = jnp.where(first_of_seq, 0.0, halo_scr[:, cols])
        halo_scr[:, cols] = qk[tm - _HALO:]
        padded = jnp.concatenate([halo, qk], axis=0)
        acc = qk * cw_ref[CONV_K - 1:CONV_K, cols] + cb_ref[:, cols]
        for j in range(1, CONV_K):
            acc = acc + pltpu.roll(padded, j, 0)[_HALO:] * cw_ref[CONV_K - 1 - j:CONV_K - j, cols]
        act = acc * _sigmoid(acc)
        if c * _CONV_CHUNK < M_QK:
            q_ref[:, cols] = act.astype(BF16)
        else:
            k_ref[:, c * _CONV_CHUNK - M_QK:(c + 1) * _CONV_CHUNK - M_QK] = (act * (MLSTM_DQK ** -0.5)).astype(BF16)

    others = [(v_ref, offs["m_v"], M_V), (o_ref, offs["m_o"], M_V), (ga_ref, offs["gate_a"], D_MODEL),
              (gb_ref, offs["gate_b"], D_MODEL)]
    for c, (ref, off, width) in enumerate(others):
        conv_chunk(c)
        ref[...] = seg(off, width).astype(BF16)
    nq_ref[...] = seg(offs["n_q"], N_Q).astype(BF16)
    narrow = seg(offs["narrow"], _NARROW_W)
    narrow_ref[...] = narrow
    narrow_t_ref[...] = narrow.T

    kv0 = offs["n_kv"]
    cmp_ref[...] = seg(kv0, 2 * N_KV)
    pos = (pl.program_id(0) % tiles_per_seq) * tm + lax.broadcasted_iota(jnp.int32, (tm, SLC_SLOTS), 0)
    onehot = (pos // SLC_BLOCK == lax.broadcasted_iota(jnp.int32, (tm, SLC_SLOTS), 1)).astype(BF16)
    kaug_ref[...] = jnp.concatenate([onehot, seg(kv0 + 2 * N_KV, N_KV).astype(BF16)], axis=1)
    for g, vt in enumerate(_values_t(seg(kv0 + 3 * N_KV, N_KV))):
        vst_ref[g] = vt
    kw_ref[...] = seg(kv0 + 4 * N_KV, N_KV).astype(BF16)
    for g, vt in enumerate(_values_t(seg(kv0 + 5 * N_KV, N_KV))):
        vwt_ref[g] = vt


def _inproj(x, norm_g, w_in, b_in, conv_w, conv_b, tm):
    B, S, _ = x.shape
    T = B * S
    G = NSA_KV_GROUPS
    cols = [w_in[:, _SEG[n][0]:_SEG[n][0] + _SEG[n][1]] for n in _WIDE]
    bias = [b_in[_SEG[n][0]:_SEG[n][0] + _SEG[n][1]] for n in _WIDE]
    narrow_w = jnp.concatenate([w_in[:, _SEG[n][0]:_SEG[n][0] + _SEG[n][1]] for n in _NARROW], axis=1)
    narrow_b = jnp.concatenate([b_in[_SEG[n][0]:_SEG[n][0] + _SEG[n][1]] for n in _NARROW])
    pad = _NARROW_W - narrow_w.shape[1]
    cols.append(jnp.pad(narrow_w, ((0, 0), (0, pad))))
    bias.append(jnp.pad(narrow_b, (0, pad)))
    w = jnp.concatenate(cols, axis=1).astype(BF16)
    b = jnp.concatenate(bias)[None, :].astype(F32)
    widths = [_SEG[n][1] for n in _WIDE] + [_NARROW_W]
    offs = dict(zip(_WIDE + ("narrow",), np.cumsum([0] + widths[:-1]).tolist()))
    n_tot = w.shape[1]
    tps = S // tm

    def rows(width, dtype):
        return pl.BlockSpec((tm, width), lambda i: (i, 0)), jax.ShapeDtypeStruct((T, width), dtype)

    def rows_t(lead, dtype):
        nd = len(lead)
        return (pl.BlockSpec((None,) + lead + (tm,), lambda i: (i // tps,) + (0,) * nd + (i % tps,)),
                jax.ShapeDtypeStruct((B,) + lead + (S,), dtype))

    outs = [rows(M_QK, BF16), rows(M_QK, BF16), rows(M_V, BF16), rows(M_V, BF16), rows(N_Q, BF16),
            rows(D_MODEL, BF16), rows(D_MODEL, BF16), rows(_NARROW_W, F32), rows_t((_NARROW_W,), F32),
            rows(2 * N_KV, F32), rows(SLC_SLOTS + N_KV, BF16), rows_t((G, _VT_ROWS), BF16), rows(N_KV, BF16),
            rows_t((G, _VT_ROWS), BF16)]
    return pl.pallas_call(
        functools.partial(_inproj_kernel, offs=offs, tiles_per_seq=tps),
        grid=(T // tm,),
        in_specs=[
            pl.BlockSpec((tm, D_MODEL), lambda i: (i, 0)),
            pl.BlockSpec((1, D_MODEL), lambda i: (0, 0)),
            pl.BlockSpec((D_MODEL, n_tot), lambda i: (0, 0)),
            pl.BlockSpec((1, n_tot), lambda i: (0, 0)),
            pl.BlockSpec((CONV_K, 2 * M_QK), lambda i: (0, 0)),
            pl.BlockSpec((1, 2 * M_QK), lambda i: (0, 0)),
        ],
        out_specs=[o[0] for o in outs],
        out_shape=[o[1] for o in outs],
        scratch_shapes=[pltpu.VMEM((_HALO, 2 * M_QK), F32)],
        compiler_params=_cparams(("arbitrary",)),
        name="inproj",
    )(x.reshape(T, D_MODEL), norm_g[None, :], w, b, conv_w, conv_b[None, :])


def _mlstm_kernel(q_ref, k_ref, v_ref, o_ref, gr_ref, gc_ref, fb_ref, fbl_ref, ng_ref, y_ref, c_scr, m_scr):
    H, DK, DV = MLSTM_HEADS, MLSTM_DQK, MLSTM_DV
    L = q_ref.shape[1]
    chains = [(bi, h) for bi in range(q_ref.shape[0]) for h in range(H)]
    heads = range(len(chains))

    @pl.when(pl.program_id(1) == 0)
    def _():
        c_scr[...] = jnp.zeros_like(c_scr)
        m_scr[...] = jnp.zeros_like(m_scr)

    r_t = lax.broadcasted_iota(jnp.int32, (L, L), 0)
    r_s = lax.broadcasted_iota(jnp.int32, (L, L), 1)
    causal = r_s <= r_t
    upper = r_t <= r_s
    gates = [gc_ref[bi] for bi in range(q_ref.shape[0])]
    lf_cols = [_log_sigmoid(g + fbl_ref[...]) for g in gates]

    qs = [q_ref[bi, :, h * DK:(h + 1) * DK] for bi, h in chains]
    ks = [k_ref[bi, :, h * DK:(h + 1) * DK] for bi, h in chains]
    qk = [_dot_nt(qs[n], ks[n]) for n in heads]
    v_ext = [jnp.concatenate([v_ref[bi, :, h * DV:(h + 1) * DV], jnp.ones((L, LANES), BF16)], axis=1)
             for bi, h in chains]
    state = [c_scr[n] for n in heads]
    inter = [_dot(qs[n], state[n].astype(BF16)) for n in heads]

    stab = []
    for n, (bi, h) in enumerate(chains):
        fb = fb_ref[h:h + 1, 0:1]
        i_row = gr_ref[bi, h:h + 1, :]
        lf_row = _log_sigmoid(gr_ref[bi, H + h:H + h + 1, :] + fb)
        i_col = gates[bi][:, h:h + 1]
        lf_col = lf_cols[bi][:, H + h:H + h + 1]
        m_prev = m_scr[n]
        b_col = jnp.sum(jnp.where(causal, lf_row, 0.0), axis=1, keepdims=True)
        b_row = jnp.sum(jnp.where(upper, lf_col, 0.0), axis=0, keepdims=True)
        b_last = jnp.sum(lf_row, axis=1, keepdims=True)
        d_intra = jnp.where(causal, b_col - b_row + i_row, NEG_BIG)
        d_inter = b_col + m_prev
        m_t = jnp.maximum(d_inter, jnp.max(d_intra, axis=1, keepdims=True))
        d_state_row = b_last - b_row + i_row
        m_new = jnp.maximum(b_last + m_prev, jnp.max(d_state_row, axis=1, keepdims=True))
        stab.append(dict(w_intra=jnp.exp(d_intra - m_t), w_inter=jnp.exp(d_inter - m_t), floor=jnp.exp(-m_t),
                         w_state=jnp.exp(b_last - b_col + i_col - m_new),
                         decay=jnp.exp(b_last + m_prev - m_new), m_new=m_new))

    s = [(qk[n] * stab[n]["w_intra"]).astype(BF16) for n in heads]
    num = [_dot(s[n], v_ext[n]) + stab[n]["w_inter"] * inter[n] for n in heads]
    for n, (bi, h) in enumerate(chains):
        cols = slice(h * DV, (h + 1) * DV)
        hid = num[n][:, :DV] / jnp.maximum(jnp.abs(num[n][:, DV:DV + 1]), stab[n]["floor"])
        hid = _rms(hid, ng_ref[:, cols])
        y_ref[bi, :, cols] = (_sigmoid(o_ref[bi, :, cols].astype(F32)) * hid).astype(y_ref.dtype)
    k_t = [ks[n].astype(F32).T.astype(BF16) for n in heads]
    vw = [v_ext[n] * stab[n]["w_state"].astype(BF16) for n in heads]
    for n in heads:
        c_scr[n] = stab[n]["decay"] * state[n] + _dot(k_t[n], vw[n])
        m_scr[n] = stab[n]["m_new"]


def _mlstm(q, k, v, o, narrow, narrow_t, f_bias, norm_g, L):
    B, S, _ = q.shape
    H, DK, DV = MLSTM_HEADS, MLSTM_DQK, MLSTM_DV
    fb = jnp.broadcast_to(f_bias.astype(F32)[:, None], (H, LANES))
    fb_lanes = jnp.zeros((1, _NARROW_W), F32).at[0, H:2 * H].set(f_bias.astype(F32))
    nb = 2 if B % 2 == 0 else 1
    return pl.pallas_call(
        _mlstm_kernel,
        grid=(B // nb, S // L),
        in_specs=[
            pl.BlockSpec((nb, L, M_QK), lambda b, c: (b, c, 0)),
            pl.BlockSpec((nb, L, M_QK), lambda b, c: (b, c, 0)),
            pl.BlockSpec((nb, L, M_V), lambda b, c: (b, c, 0)),
            pl.BlockSpec((nb, L, M_V), lambda b, c: (b, c, 0)),
            pl.BlockSpec((nb, 2 * H, L), lambda b, c: (b, 0, c)),
            pl.BlockSpec((nb, L, _NARROW_W), lambda b, c: (b, c, 0)),
            pl.BlockSpec((H, LANES), lambda b, c: (0, 0)),
            pl.BlockSpec((1, _NARROW_W), lambda b, c: (0, 0)),
            pl.BlockSpec((1, M_V), lambda b, c: (0, 0)),
        ],
        out_specs=pl.BlockSpec((nb, L, M_V), lambda b, c: (b, c, 0)),
        out_shape=jax.ShapeDtypeStruct((B, S, M_V), BF16),
        scratch_shapes=[pltpu.VMEM((nb * H, DK, DV + LANES), F32), pltpu.VMEM((nb * H, 1, 1), F32)],
        compiler_params=_cparams(("parallel", "arbitrary")),
        name="mlstm",
    )(q, k, v, o, narrow_t, narrow, fb, fb_lanes, norm_g[None, :])


def _compress_kernel(x_ref, pos_ref, w1_ref, w2_ref, out_ref, out_t_ref, x_scr):
    S = x_ref.shape[0]
    n = S // CMP_STRIDE
    x_scr[0:S] = x_ref[...]
    x_scr[S:] = jnp.zeros((CMP_STRIDE, LANES), F32)
    acc = jnp.zeros((n, w1_ref.shape[2]), F32)
    for l in range(CMP_BLOCK):
        tok = x_scr[pl.ds(l, n, stride=CMP_STRIDE), :] + pos_ref[l:l + 1, :]
        acc = acc + _dot(tok.astype(BF16), w1_ref[l])
    hid = acc * _sigmoid(acc)
    out = _dot(hid.astype(BF16), w2_ref[...])
    row = lax.broadcasted_iota(jnp.int32, out.shape, 0)
    out = jnp.where(row < n - 1, out, 0.0)
    out_ref[...] = out.astype(out_ref.dtype)
    out_t_ref[...] = out.T.astype(out_t_ref.dtype)


def _block_diag2(w):
    z = jnp.zeros_like(w)
    return jnp.concatenate([jnp.concatenate([w, z], axis=-1), jnp.concatenate([z, w], axis=-1)], axis=-2)


def _compress(cmp_kv, pos, w1, w2):
    B, S, _ = cmp_kv.shape
    n = S // CMP_STRIDE
    pos2 = jnp.concatenate([pos, pos], axis=-1)
    w1b = _block_diag2(w1.reshape(2, CMP_BLOCK, NSA_DH, CMP_HIDDEN)).astype(BF16)
    w2b = _block_diag2(w2).astype(BF16)
    return pl.pallas_call(
        _compress_kernel,
        grid=(B, 2),
        in_specs=[
            pl.BlockSpec((None, S, LANES), lambda b, a: (b, 0, a)),
            pl.BlockSpec((None, CMP_BLOCK, LANES), lambda b, a: (a, 0, 0)),
            pl.BlockSpec((None, CMP_BLOCK, LANES, 2 * CMP_HIDDEN), lambda b, a: (a, 0, 0, 0)),
            pl.BlockSpec((None, 2 * CMP_HIDDEN, LANES), lambda b, a: (a, 0, 0)),
        ],
        out_specs=[pl.BlockSpec((None, None, n, LANES), lambda b, a: (b, a, 0, 0)),
                   pl.BlockSpec((None, None, LANES, n), lambda b, a: (b, a, 0, 0))],
        out_shape=[jax.ShapeDtypeStruct((B, 2, n, LANES), BF16),
                   jax.ShapeDtypeStruct((B, 2, LANES, n), BF16)],
        scratch_shapes=[pltpu.VMEM((S + CMP_STRIDE, LANES), F32)],
        compiler_params=_cparams(("parallel", "parallel")),
        name="nsa_compress",
    )(cmp_kv, pos2, w1b, w2b)


_CMP_PER_SLC = SLC_BLOCK // CMP_STRIDE
_FORCED = 3


def _group_rows(q_h, g):
    z = jnp.zeros_like(q_h)
    return jnp.concatenate([q_h, z] if g == 0 else [z, q_h], axis=0)


def _select_kernel(q_ref, kc_ref, vct_ref, qt_ref, ocmp_ref, mask_ref, ps_scr):
    tq = q_ref.shape[0]
    t0 = pl.program_id(1) * tq
    q_t = (q_ref[...].astype(F32) * (NSA_DH ** -0.5)).T.astype(BF16)
    qt_ref[...] = q_t
    need = (t0 + tq) // CMP_STRIDE
    for nk in range(LANES, kc_ref.shape[0] + 1, LANES):
        pl.when((need - 1) // LANES == nk // LANES - 1)(
            functools.partial(_select_body, q_t, kc_ref, vct_ref, ocmp_ref, mask_ref, ps_scr, t0, nk))


def _select_body(q_t, kc_ref, vct_ref, ocmp_ref, mask_ref, ps_scr, t0, ncp):
    tq = q_t.shape[1]
    n_slc = ncp // _CMP_PER_SLC
    t_row = t0 + lax.broadcasted_iota(jnp.int32, (1, tq), 1)
    c_col = lax.broadcasted_iota(jnp.int32, (ncp, 1), 0)
    visible = (c_col * CMP_STRIDE + (CMP_BLOCK - 1)) <= t_row
    any_visible = t_row >= CMP_BLOCK - 1

    blk = lax.broadcasted_iota(jnp.int32, (n_slc, tq), 0)
    cur = (t0 + lax.broadcasted_iota(jnp.int32, (n_slc, tq), 1)) // SLC_BLOCK
    forced = (blk == 0) | (blk == cur) | (blk == cur - 1)
    eligible = blk <= cur

    def scores(hd):
        return _dot(kc_ref[0:ncp, :], _group_rows(q_t[hd * NSA_DH:(hd + 1) * NSA_DH], hd // NSA_HPG))

    s_next = scores(0)
    for g in range(NSA_KV_GROUPS):
        vct = vct_ref[g * NSA_DH:(g + 1) * NSA_DH, 0:ncp]
        for hh in range(NSA_HPG):
            hd = g * NSA_HPG + hh
            s = jnp.where(visible, s_next, NEG_BIG)
            if hd + 1 < NSA_HEADS:
                s_next = scores(hd + 1)
            e = jnp.exp(s - jnp.max(s, axis=0, keepdims=True))
            inv = jnp.where(any_visible, 1.0 / jnp.sum(e, axis=0, keepdims=True), 0.0)
            p = e * inv
            ocmp_ref[hd * NSA_DH:(hd + 1) * NSA_DH, :] = _dot(vct, p.astype(BF16)).astype(ocmp_ref.dtype)
            for c in range(tq // LANES):
                if hh == 0:
                    ps_scr[c, 0:ncp, :] = p[:, c * LANES:(c + 1) * LANES]
                else:
                    ps_scr[c, 0:ncp, :] += p[:, c * LANES:(c + 1) * LANES]
        lanes = [jnp.concatenate([ps_scr[c, pl.ds(r, n_slc, stride=_CMP_PER_SLC), :]
                                  for c in range(tq // LANES)], axis=1) for r in range(_CMP_PER_SLC)]
        row = lax.broadcasted_iota(jnp.int32, (n_slc, tq), 0)
        before = jnp.where(row == 0, 0.0, pltpu.roll(lanes[3], 1, 0))
        imp = before + 2.0 * (lanes[0] + lanes[1] + lanes[2]) + lanes[3]
        score = jnp.where(forced, -jnp.inf, jnp.where(eligible, imp, -SEL_BIG))
        for _ in range(SLC_TOPN - _FORCED):
            best = jnp.max(score, axis=0, keepdims=True)
            first = jnp.min(jnp.where(score == best, blk, SLC_SLOTS), axis=0, keepdims=True)
            score = jnp.where(blk == first, -jnp.inf, score)
        mask_ref[g, 0:n_slc, :] = jnp.where(score == -jnp.inf, 0.0, -MASK_BIG).astype(mask_ref.dtype)
        if n_slc < SLC_SLOTS:
            mask_ref[g, n_slc:, :] = jnp.full((SLC_SLOTS - n_slc, tq), -MASK_BIG, mask_ref.dtype)


def _select(n_q, cmp, cmp_t, tq):
    B, S, _ = n_q.shape
    G = NSA_KV_GROUPS
    ncp = cmp.shape[2]
    return pl.pallas_call(
        _select_kernel,
        grid=(B, S // tq),
        in_specs=[
            pl.BlockSpec((None, tq, N_Q), lambda b, i: (b, i, 0)),
            pl.BlockSpec((None, None, ncp, LANES), lambda b, i: (b, 0, 0, 0)),
            pl.BlockSpec((None, None, LANES, ncp), lambda b, i: (b, 1, 0, 0)),
        ],
        out_specs=[
            pl.BlockSpec((None, N_Q, tq), lambda b, i: (b, 0, i)),
            pl.BlockSpec((None, N_Q, tq), lambda b, i: (b, 0, i)),
            pl.BlockSpec((None, G, SLC_SLOTS, tq), lambda b, i: (b, 0, 0, i)),
        ],
        out_shape=[
            jax.ShapeDtypeStruct((B, N_Q, S), BF16),
            jax.ShapeDtypeStruct((B, N_Q, S), BF16),
            jax.ShapeDtypeStruct((B, G, SLC_SLOTS, S), BF16),
        ],
        scratch_shapes=[pltpu.VMEM((tq // LANES, ncp, LANES), F32)],
        compiler_params=_cparams(("parallel", "parallel")),
        name="nsa_select",
    )(n_q, cmp, cmp_t)


_TK = 256
_KAUG_W = SLC_SLOTS + N_KV
_AHEAD = 7
_TILES_PER_TRIP = 8


def _pipelined(work, score, consume):
    pending = [score(item) for item in work[:_AHEAD]]
    for i, item in enumerate(work):
        s = pending.pop(0)
        if i + _AHEAD < len(work):
            pending.append(score(work[i + _AHEAD]))
        consume(item, s)


def _attend_kernel(qt_ref, mask_ref, ocmp_ref, gate_ref, kaug_ref, vt_ref, kw_ref, vwt_ref, y_ref,
                   qa_scr, m_scr, acc_scr, yt_scr):
    tq = qt_ref.shape[1]
    S = kaug_ref.shape[0]
    G = NSA_KV_GROUPS
    q0 = pl.program_id(1) * tq
    t_q = q0 + lax.broadcasted_iota(jnp.int32, (1, tq), 1)
    n_full = q0 // _TK
    kd = pl.multiple_of(n_full * _TK, _TK)
    causal = (kd + lax.broadcasted_iota(jnp.int32, (_TK, 1), 0)) <= t_q
    heads = [(g, hh) for g in range(G) for hh in range(NSA_HPG)]

    for g, hh in heads:
        hd = g * NSA_HPG + hh
        cols = slice(hh * tq, (hh + 1) * tq)
        qa_scr[g, 0:SLC_SLOTS, cols] = mask_ref[g]
        qa_scr[g, SLC_SLOTS:, cols] = _group_rows(qt_ref[hd * NSA_DH:(hd + 1) * NSA_DH, :], g)

    m_scr[...] = jnp.full(m_scr.shape, NEG_BIG, F32)
    acc_scr[...] = jnp.zeros_like(acc_scr)

    def selected_scores(item):
        (g, hh), k0 = item
        return _dot(kaug_ref[pl.ds(k0, _TK), :], qa_scr[g, :, hh * tq:(hh + 1) * tq])

    def selected_tiles(starts, diagonal):
        def consume(item, s):
            (g, hh), k0 = item
            cols = slice(hh * tq, (hh + 1) * tq)
            if diagonal:
                s = jnp.where(causal, s, -MASK_BIG)
            m_old = m_scr[g, :, cols]
            m_new = jnp.maximum(m_old, jnp.max(s, axis=0, keepdims=True).astype(BF16).astype(F32))
            p = jnp.exp(s.astype(BF16) - m_new[0:1].astype(BF16))
            acc_scr[g, :, cols] = (jnp.exp(m_old - m_new)[0:1] * acc_scr[g, :, cols]
                                   + _dot(vt_ref[g, :, pl.ds(k0, _TK)], p))
            m_scr[g, :, cols] = m_new
        _pipelined([(head, k0) for k0 in starts for head in heads], selected_scores, consume)

    def run_tiles(first, count):
        selected_tiles([pl.multiple_of(first + n * _TK, _TK) for n in range(count)], False)

    def stretch(j, carry):
        run_tiles(j * (_TILES_PER_TRIP * _TK), _TILES_PER_TRIP)
        return carry

    lax.fori_loop(0, n_full // _TILES_PER_TRIP, stretch, 0)
    count = _TILES_PER_TRIP // 2
    while count:
        pl.when(n_full & count != 0)(functools.partial(
            run_tiles, (n_full - n_full % (2 * count)) * _TK, count))
        count //= 2
    selected_tiles([kd], True)

    gates_t = _sigmoid(gate_ref[...]).T
    wlen = min(tq + WINDOW, S)
    w0 = pl.multiple_of(jnp.maximum(q0 - WINDOW, 0), tq)

    n_wt = wlen // _TK
    w_tile = [pl.multiple_of(w0 + j * _TK, _TK) for j in range(n_wt)]

    def window_scores(g, hh):
        q_cols = qa_scr[g, SLC_SLOTS:, hh * tq:(hh + 1) * tq]
        return [_dot(kw_ref[pl.ds(w_tile[j], _TK), :], q_cols) for j in range(n_wt)]

    def window(band_mask):
        s_next = window_scores(*heads[0])
        for i, (g, hh) in enumerate(heads):
            s = band_mask(jnp.concatenate(s_next, axis=0))
            if i + 1 < len(heads):
                s_next = window_scores(*heads[i + 1])
            hd = g * NSA_HPG + hh
            rows_h = slice(hd * NSA_DH, (hd + 1) * NSA_DH)
            e = jnp.exp(s.astype(BF16) - jnp.max(s, axis=0, keepdims=True).astype(BF16))
            ow = sum(_dot(vwt_ref[g, :, pl.ds(w_tile[j], _TK)], e[j * _TK:(j + 1) * _TK])
                     for j in range(n_wt))
            acc = acc_scr[g, :, hh * tq:(hh + 1) * tq]
            gr = GATE_COL0 + 3 * hd
            yt_scr[rows_h, :] = (gates_t[gr:gr + 1] * ocmp_ref[rows_h, :].astype(F32)
                                 + gates_t[gr + 1:gr + 2] * (acc[:NSA_DH] / acc[NSA_DH:NSA_DH + 1])
                                 + gates_t[gr + 2:gr + 3] * (ow[:NSA_DH] / ow[NSA_DH:NSA_DH + 1]))

    def general_band(s):
        kpos = w0 + lax.broadcasted_iota(jnp.int32, (wlen, 1), 0)
        return jnp.where((kpos <= t_q) & (kpos > t_q - WINDOW), s, NEG_BIG)

    if wlen == tq + WINDOW and WINDOW % tq == 0:
        r = lax.broadcasted_iota(jnp.int32, (tq, tq), 0)
        c = lax.broadcasted_iota(jnp.int32, (tq, tq), 1)

        def interior_band(s):
            return jnp.concatenate([jnp.where(r > c, s[:tq], NEG_BIG), s[tq:WINDOW],
                                    jnp.where(r <= c, s[WINDOW:], NEG_BIG)], axis=0)

        pl.when(q0 >= WINDOW)(functools.partial(window, interior_band))
        pl.when(q0 < WINDOW)(functools.partial(window, general_band))
    else:
        window(general_band)
    y_ref[...] = yt_scr[...].T.astype(y_ref.dtype)


def _attend(q_t, mask_t, ocmp_t, narrow, kaug, v_t, kw, vw_t, tq):
    B, _, S = q_t.shape
    G = NSA_KV_GROUPS
    rows = NSA_HPG * tq
    return pl.pallas_call(
        _attend_kernel,
        grid=(B, S // tq),
        in_specs=[
            pl.BlockSpec((None, N_Q, tq), lambda b, i: (b, 0, i)),
            pl.BlockSpec((None, G, SLC_SLOTS, tq), lambda b, i: (b, 0, 0, i)),
            pl.BlockSpec((None, N_Q, tq), lambda b, i: (b, 0, i)),
            pl.BlockSpec((None, tq, _NARROW_W), lambda b, i: (b, i, 0)),
            pl.BlockSpec((None, S, _KAUG_W), lambda b, i: (b, 0, 0)),
            pl.BlockSpec((None, G, _VT_ROWS, S), lambda b, i: (b, 0, 0, 0)),
            pl.BlockSpec((None, S, N_KV), lambda b, i: (b, 0, 0)),
            pl.BlockSpec((None, G, _VT_ROWS, S), lambda b, i: (b, 0, 0, 0)),
        ],
        out_specs=pl.BlockSpec((None, tq, N_Q), lambda b, i: (b, i, 0)),
        out_shape=jax.ShapeDtypeStruct((B, S, N_Q), BF16),
        scratch_shapes=[
            pltpu.VMEM((G, _KAUG_W, rows), BF16),
            pltpu.VMEM((G, 8, rows), F32),
            pltpu.VMEM((G, _VT_ROWS, rows), F32),
            pltpu.VMEM((N_Q, tq), F32),
        ],
        compiler_params=_cparams(("parallel", "arbitrary")),
        name="nsa_attend",
    )(q_t, mask_t, ocmp_t, narrow, kaug, v_t, kw, vw_t)


def _merge_kernel(x_ref, ya_ref, yb_ref, ga_ref, gb_ref, wa_ref, wb_ref, wo_ref, g2_ref, x1_ref, h2_ref):
    a = _sigmoid(ga_ref[...].astype(F32)) * _dot(ya_ref[...], wa_ref[...])
    b = _sigmoid(gb_ref[...].astype(F32)) * _dot(yb_ref[...], wb_ref[...])
    x1 = x_ref[...] + _dot((a + b).astype(BF16), wo_ref[...])
    x1_ref[...] = x1
    h2_ref[...] = _rms(x1, g2_ref[...]).astype(h2_ref.dtype)


def _merge(x2d, ya, yb, ga, gb, wa, wb, wo, g2, tm):
    T = x2d.shape[0]
    row = lambda w: pl.BlockSpec((tm, w), lambda i: (i, 0))
    const = lambda r, c: pl.BlockSpec((r, c), lambda i: (0, 0))
    return pl.pallas_call(
        _merge_kernel,
        grid=(T // tm,),
        in_specs=[row(D_MODEL), row(M_V), row(N_Q), row(D_MODEL), row(D_MODEL),
                  const(M_V, D_MODEL), const(N_Q, D_MODEL), const(D_MODEL, D_MODEL), const(1, D_MODEL)],
        out_specs=[row(D_MODEL), row(D_MODEL)],
        out_shape=[jax.ShapeDtypeStruct((T, D_MODEL), F32), jax.ShapeDtypeStruct((T, D_MODEL), BF16)],
        compiler_params=_cparams(("parallel",)),
        name="merge",
    )(x2d, ya, yb, ga, gb, wa.astype(BF16), wb.astype(BF16), wo.astype(BF16), g2[None, :])


_FFN_CHUNK = 256


def _ffn_kernel(x1_ref, h2_ref, wg_ref, wu_ref, wd_ref, gf_ref, out_ref):
    h = h2_ref[...]
    acc = x1_ref[...]
    for c in range(FFN_HIDDEN // _FFN_CHUNK):
        sl = slice(c * _FFN_CHUNK, (c + 1) * _FFN_CHUNK)
        gate = _dot(h, wg_ref[:, sl])
        up = _dot(h, wu_ref[:, sl])
        acc = acc + _dot((gate * _sigmoid(gate) * up).astype(BF16), wd_ref[sl, :])
    out_ref[...] = _rms(acc, gf_ref[...])


def _ffn(x1, h2, wg, wu, wd, gf, tm):
    T = x1.shape[0]
    row = lambda: pl.BlockSpec((tm, D_MODEL), lambda i: (i, 0))
    const = lambda r, c: pl.BlockSpec((r, c), lambda i: (0, 0))
    return pl.pallas_call(
        _ffn_kernel,
        grid=(T // tm,),
        in_specs=[row(), row(), const(D_MODEL, FFN_HIDDEN), const(D_MODEL, FFN_HIDDEN),
                  const(FFN_HIDDEN, D_MODEL), const(1, D_MODEL)],
        out_specs=row(),
        out_shape=jax.ShapeDtypeStruct((T, D_MODEL), F32),
        compiler_params=_cparams(("parallel",)),
        name="ffn",
    )(x1, h2, wg.astype(BF16), wu.astype(BF16), wd.astype(BF16), gf[None, :])


def _layer(x, norm1_g, w_in, b_in, f_bias, conv_w, conv_b, mlstm_norm_g, cmp_k_pos, cmp_k_w1, cmp_k_w2,
           cmp_v_pos, cmp_v_w1, cmp_v_w2, w_branch_a, w_branch_b, w_out, norm2_g):
    B, S, D = x.shape
    T = B * S
    tm = min(256, S)

    (m_q, m_k, m_v, m_o, n_q, gate_a, gate_b, narrow, narrow_t, cmp_kv, kaug, vs_t, k_win, vw_t) = _inproj(
        x, norm1_g, w_in, b_in, conv_w, conv_b, tm)
    narrow = narrow.reshape(B, S, _NARROW_W)

    y_a = _mlstm(m_q.reshape(B, S, M_QK), m_k.reshape(B, S, M_QK), m_v.reshape(B, S, M_V),
                 m_o.reshape(B, S, M_V), narrow, narrow_t, f_bias, mlstm_norm_g, min(256, S))

    cmp, cmp_t = _compress(cmp_kv.reshape(B, S, 2 * N_KV), jnp.stack([cmp_k_pos, cmp_v_pos]),
                           jnp.stack([cmp_k_w1, cmp_v_w1]), jnp.stack([cmp_k_w2, cmp_v_w2]))
    tq = min(256, S)
    q_t, ocmp_t, mask_t = _select(n_q.reshape(B, S, N_Q), cmp, cmp_t, tq)
    y_b = _attend(q_t, mask_t, ocmp_t, narrow, kaug.reshape(B, S, _KAUG_W), vs_t,
                  k_win.reshape(B, S, N_KV), vw_t, tq)

    return _merge(x.reshape(T, D), y_a.reshape(T, M_V), y_b.reshape(T, N_Q), gate_a, gate_b,
                  w_branch_a, w_branch_b, w_out, norm2_g, min(512, T))


def kernel(x, norm1_g, w_in, b_in, f_bias, conv_w, conv_b, mlstm_norm_g, cmp_k_pos, cmp_k_w1, cmp_k_w2,
           cmp_v_pos, cmp_v_w1, cmp_v_w2, w_branch_a, w_branch_b, w_out, norm2_g, w_ffn_gate, w_ffn_up,
           w_ffn_down, norm_f_g):
    B, S, D = x.shape
    depth = w_in.shape[0]
    assert depth == 1, "the fused final norm assumes a single layer"
    x1, h2 = _layer(x, norm1_g[0], w_in[0], b_in[0], f_bias[0], conv_w[0], conv_b[0], mlstm_norm_g[0],
                    cmp_k_pos[0], cmp_k_w1[0], cmp_k_w2[0], cmp_v_pos[0], cmp_v_w1[0], cmp_v_w2[0],
                    w_branch_a[0], w_branch_b[0], w_out[0], norm2_g[0])
    out = _ffn(x1, h2, w_ffn_gate[0], w_ffn_up[0], w_ffn_down[0], norm_f_g, min(512, B * S))
    return out.reshape(B, S, D)
```

```python
import functools

import numpy as np
import jax
import jax.numpy as jnp
from jax import lax
from jax.experimental import pallas as pl
from jax.experimental.pallas import tpu as pltpu

F32 = jnp.float32
BF16 = jnp.bfloat16

D_MODEL = 1024
MLSTM_HEADS = 4
MLSTM_DV = D_MODEL // MLSTM_HEADS
MLSTM_DQK = MLSTM_DV // 2
CONV_K = 4
NSA_DH = 64
NSA_HEADS = (D_MODEL // 2) // NSA_DH
NSA_KV_GROUPS = 2
NSA_HPG = NSA_HEADS // NSA_KV_GROUPS
CMP_BLOCK = 32
CMP_STRIDE = 16
CMP_HIDDEN = 256
SLC_BLOCK = 64
SLC_TOPN = 16
WINDOW = 512
FFN_HIDDEN = 2816
RMS_EPS = 1e-6
SEL_BIG = 1e9

M_QK = MLSTM_HEADS * MLSTM_DQK
M_V = MLSTM_HEADS * MLSTM_DV
N_Q = NSA_HEADS * NSA_DH
N_KV = NSA_KV_GROUPS * NSA_DH
IN_SPLITS = (2 * M_QK, M_V, M_V, MLSTM_HEADS, MLSTM_HEADS, N_Q, 6 * N_KV, 3 * NSA_HEADS, D_MODEL, D_MODEL)

LANES = 128
SLC_SLOTS = 128
MASK_BIG = float(2.0 ** 100)
NEG_BIG = -1e30
VMEM_LIMIT = 56 * 1024 * 1024
assert N_KV == LANES, "both KV groups are packed into one 128-lane key row"

_SEG = dict(zip(("m_qk", "m_v", "m_o", "m_i", "m_f", "n_q", "n_kv", "n_g", "gate_a", "gate_b"),
                zip(np.cumsum((0,) + IN_SPLITS[:-1]).tolist(), IN_SPLITS)))
_WIDE = ("m_qk", "m_v", "m_o", "n_q", "n_kv", "gate_a", "gate_b")
_NARROW = ("m_i", "m_f", "n_g")
_NARROW_W = LANES
GATE_COL0 = 2 * MLSTM_HEADS
_VT_ROWS = 80


def _cparams(sem):
    return pltpu.CompilerParams(dimension_semantics=sem, vmem_limit_bytes=VMEM_LIMIT)


def _sigmoid(x):
    return 1.0 / (1.0 + jnp.exp(-x))


def _log_sigmoid(x):
    return jnp.minimum(x, 0.0) - jnp.log(1.0 + jnp.exp(-jnp.abs(x)))


def _rms(x, g):
    ms = jnp.mean(x * x, axis=-1, keepdims=True)
    return x * lax.rsqrt(ms + RMS_EPS) * g


def _dot_nt(a, b):
    return lax.dot_general(a, b, (((1,), (1,)), ((), ())), preferred_element_type=F32)


def _dot(a, b):
    return jnp.dot(a, b, preferred_element_type=F32)


def _values_t(v):
    v_t = v.T
    row = lax.broadcasted_iota(jnp.int32, (_VT_ROWS - NSA_DH, v.shape[0]), 0)
    tail = jnp.where(row == 0, 1.0, 0.0)
    return [jnp.concatenate([v_t[g * NSA_DH:(g + 1) * NSA_DH], tail], axis=0).astype(BF16)
            for g in range(NSA_KV_GROUPS)]


_HALO = 8
_CONV_CHUNK = 2 * M_QK // 4
assert M_QK % _CONV_CHUNK == 0


def _inproj_kernel(x_ref, g_ref, w_ref, b_ref, cw_ref, cb_ref, q_ref, k_ref, v_ref, o_ref, nq_ref, ga_ref,
                   gb_ref, narrow_ref, narrow_t_ref, cmp_ref, kaug_ref, vst_ref, kw_ref, vwt_ref, halo_scr,
                   *, offs, tiles_per_seq):
    tm = x_ref.shape[0]
    first_of_seq = pl.program_id(0) % tiles_per_seq == 0
    h = _rms(x_ref[...], g_ref[...]).astype(BF16)

    def seg(off, width):
        return _dot(h, w_ref[:, off:off + width]) + b_ref[:, off:off + width]

    def conv_chunk(c):
        cols = slice(c * _CONV_CHUNK, (c + 1) * _CONV_CHUNK)
        qk = seg(offs["m_qk"] + c * _CONV_CHUNK, _CONV_CHUNK)
        halo = jnp.where(first_of_seq, 0.0, halo_scr[:, cols])
        halo_scr[:, cols] = qk[tm - _HALO:]
        padded = jnp.concatenate([halo, qk], axis=0)
        acc = qk * cw_ref[CONV_K - 1:CONV_K, cols] + cb_ref[:, cols]
        for j in range(1, CONV_K):
            acc = acc + pltpu.roll(padded, j, 0)[_HALO:] * cw_ref[CONV_K - 1 - j:CONV_K - j, cols]
        act = acc * _sigmoid(acc)
        if c * _CONV_CHUNK < M_QK:
            q_ref[:, cols] = act.astype(BF16)
        else:
            k_ref[:, c * _CONV_CHUNK - M_QK:(c + 1) * _CONV_CHUNK - M_QK] = (act * (MLSTM_DQK ** -0.5)).astype(BF16)

    others = [(v_ref, offs["m_v"], M_V), (o_ref, offs["m_o"], M_V), (ga_ref, offs["gate_a"], D_MODEL),
              (gb_ref, offs["gate_b"], D_MODEL)]
    per = 2 * M_QK // _CONV_CHUNK // len(others)
    for n, (ref, off, width) in enumerate(others):
        for c in range(n * per, (n + 1) * per):
            conv_chunk(c)
        ref[...] = seg(off, width).astype(BF16)
    nq_ref[...] = seg(offs["n_q"], N_Q).astype(BF16)
    narrow = seg(offs["narrow"], _NARROW_W)
    narrow_ref[...] = narrow
    narrow_t_ref[...] = narrow.T

    kv0 = offs["n_kv"]
    cmp_ref[...] = seg(kv0, 2 * N_KV)
    pos = (pl.program_id(0) % tiles_per_seq) * tm + lax.broadcasted_iota(jnp.int32, (tm, SLC_SLOTS), 0)
    onehot = (pos // SLC_BLOCK == lax.broadcasted_iota(jnp.int32, (tm, SLC_SLOTS), 1)).astype(BF16)
    kaug_ref[...] = jnp.concatenate([onehot, seg(kv0 + 2 * N_KV, N_KV).astype(BF16)], axis=1)
    for g, vt in enumerate(_values_t(seg(kv0 + 3 * N_KV, N_KV))):
        vst_ref[g] = vt
    kw_ref[...] = seg(kv0 + 4 * N_KV, N_KV).astype(BF16)
    for g, vt in enumerate(_values_t(seg(kv0 + 5 * N_KV, N_KV))):
        vwt_ref[g] = vt


def _inproj(x, norm_g, w_in, b_in, conv_w, conv_b, tm):
    B, S, _ = x.shape
    T = B * S
    G = NSA_KV_GROUPS
    cols = [w_in[:, _SEG[n][0]:_SEG[n][0] + _SEG[n][1]] for n in _WIDE]
    bias = [b_in[_SEG[n][0]:_SEG[n][0] + _SEG[n][1]] for n in _WIDE]
    narrow_w = jnp.concatenate([w_in[:, _SEG[n][0]:_SEG[n][0] + _SEG[n][1]] for n in _NARROW], axis=1)
    narrow_b = jnp.concatenate([b_in[_SEG[n][0]:_SEG[n][0] + _SEG[n][1]] for n in _NARROW])
    pad = _NARROW_W - narrow_w.shape[1]
    cols.append(jnp.pad(narrow_w, ((0, 0), (0, pad))))
    bias.append(jnp.pad(narrow_b, (0, pad)))
    w = jnp.concatenate(cols, axis=1).astype(BF16)
    b = jnp.concatenate(bias)[None, :].astype(F32)
    widths = [_SEG[n][1] for n in _WIDE] + [_NARROW_W]
    offs = dict(zip(_WIDE + ("narrow",), np.cumsum([0] + widths[:-1]).tolist()))
    n_tot = w.shape[1]
    tps = S // tm

    def rows(width, dtype):
        return pl.BlockSpec((tm, width), lambda i: (i, 0)), jax.ShapeDtypeStruct((T, width), dtype)

    def rows_t(lead, dtype):
        nd = len(lead)
        return (pl.BlockSpec((None,) + lead + (tm,), lambda i: (i // tps,) + (0,) * nd + (i % tps,)),
                jax.ShapeDtypeStruct((B,) + lead + (S,), dtype))

    outs = [rows(M_QK, BF16), rows(M_QK, BF16), rows(M_V, BF16), rows(M_V, BF16), rows(N_Q, BF16),
            rows(D_MODEL, BF16), rows(D_MODEL, BF16), rows(_NARROW_W, F32), rows_t((_NARROW_W,), F32),
            rows(2 * N_KV, F32), rows(SLC_SLOTS + N_KV, BF16), rows_t((G, _VT_ROWS), BF16), rows(N_KV, BF16),
            rows_t((G, _VT_ROWS), BF16)]
    return pl.pallas_call(
        functools.partial(_inproj_kernel, offs=offs, tiles_per_seq=tps),
        grid=(T // tm,),
        in_specs=[
            pl.BlockSpec((tm, D_MODEL), lambda i: (i, 0)),
            pl.BlockSpec((1, D_MODEL), lambda i: (0, 0)),
            pl.BlockSpec((D_MODEL, n_tot), lambda i: (0, 0)),
            pl.BlockSpec((1, n_tot), lambda i: (0, 0)),
            pl.BlockSpec((CONV_K, 2 * M_QK), lambda i: (0, 0)),
            pl.BlockSpec((1, 2 * M_QK), lambda i: (0, 0)),
        ],
        out_specs=[o[0] for o in outs],
        out_shape=[o[1] for o in outs],
        scratch_shapes=[pltpu.VMEM((_HALO, 2 * M_QK), F32)],
        compiler_params=_cparams(("arbitrary",)),
        name="inproj",
    )(x.reshape(T, D_MODEL), norm_g[None, :], w, b, conv_w, conv_b[None, :])


def _mlstm_kernel(q_ref, k_ref, v_ref, o_ref, gr_ref, gc_ref, fb_ref, fbl_ref, ng_ref, y_ref, c_scr, m_scr):
    H, DK, DV = MLSTM_HEADS, MLSTM_DQK, MLSTM_DV
    L = q_ref.shape[1]
    chains = [(bi, h) for bi in range(q_ref.shape[0]) for h in range(H)]
    heads = range(len(chains))

    @pl.when(pl.program_id(1) == 0)
    def _():
        c_scr[...] = jnp.zeros_like(c_scr)
        m_scr[...] = jnp.zeros_like(m_scr)

    r_t = lax.broadcasted_iota(jnp.int32, (L, L), 0)
    r_s = lax.broadcasted_iota(jnp.int32, (L, L), 1)
    causal = r_s <= r_t
    upper = r_t <= r_s
    gates = [gc_ref[bi] for bi in range(q_ref.shape[0])]
    lf_cols = [_log_sigmoid(g + fbl_ref[...]) for g in gates]

    qs = [q_ref[bi, :, h * DK:(h + 1) * DK] for bi, h in chains]
    ks = [k_ref[bi, :, h * DK:(h + 1) * DK] for bi, h in chains]
    qk = [_dot_nt(qs[n], ks[n]) for n in heads]
    v_ext = [jnp.concatenate([v_ref[bi, :, h * DV:(h + 1) * DV], jnp.ones((L, LANES), BF16)], axis=1)
             for bi, h in chains]
    state = [c_scr[n] for n in heads]
    inter = [_dot(qs[n], state[n].astype(BF16)) for n in heads]

    stab = []
    for n, (bi, h) in enumerate(chains):
        fb = fb_ref[h:h + 1, 0:1]
        i_row = gr_ref[bi, h:h + 1, :]
        lf_row = _log_sigmoid(gr_ref[bi, H + h:H + h + 1, :] + fb)
        i_col = gates[bi][:, h:h + 1]
        lf_col = lf_cols[bi][:, H + h:H + h + 1]
        m_prev = m_scr[n]
        b_col = jnp.sum(jnp.where(causal, lf_row, 0.0), axis=1, keepdims=True)
        b_row = jnp.sum(jnp.where(upper, lf_col, 0.0), axis=0, keepdims=True)
        b_last = jnp.sum(lf_row, axis=1, keepdims=True)
        d_intra = jnp.where(causal, b_col - b_row + i_row, NEG_BIG)
        d_inter = b_col + m_prev
        m_t = jnp.maximum(d_inter, jnp.max(d_intra, axis=1, keepdims=True))
        d_state_row = b_last - b_row + i_row
        m_new = jnp.maximum(b_last + m_prev, jnp.max(d_state_row, axis=1, keepdims=True))
        stab.append(dict(w_intra=jnp.exp(d_intra - m_t), w_inter=jnp.exp(d_inter - m_t), floor=jnp.exp(-m_t),
                         w_state=jnp.exp(b_last - b_col + i_col - m_new),
                         decay=jnp.exp(b_last + m_prev - m_new), m_new=m_new))

    s = [(qk[n] * stab[n]["w_intra"]).astype(BF16) for n in heads]
    num = [_dot(s[n], v_ext[n]) + stab[n]["w_inter"] * inter[n] for n in heads]
    for n, (bi, h) in enumerate(chains):
        cols = slice(h * DV, (h + 1) * DV)
        hid = num[n][:, :DV] / jnp.maximum(jnp.abs(num[n][:, DV:DV + 1]), stab[n]["floor"])
        hid = _rms(hid, ng_ref[:, cols])
        y_ref[bi, :, cols] = (_sigmoid(o_ref[bi, :, cols].astype(F32)) * hid).astype(y_ref.dtype)
    k_t = [ks[n].astype(F32).T.astype(BF16) for n in heads]
    vw = [v_ext[n] * stab[n]["w_state"].astype(BF16) for n in heads]
    for n in heads:
        c_scr[n] = stab[n]["decay"] * state[n] + _dot(k_t[n], vw[n])
        m_scr[n] = stab[n]["m_new"]


def _mlstm(q, k, v, o, narrow, narrow_t, f_bias, norm_g, L):
    B, S, _ = q.shape
    H, DK, DV = MLSTM_HEADS, MLSTM_DQK, MLSTM_DV
    fb = jnp.broadcast_to(f_bias.astype(F32)[:, None], (H, LANES))
    fb_lanes = jnp.zeros((1, _NARROW_W), F32).at[0, H:2 * H].set(f_bias.astype(F32))
    nb = 2 if B % 2 == 0 else 1
    return pl.pallas_call(
        _mlstm_kernel,
        grid=(B // nb, S // L),
        in_specs=[
            pl.BlockSpec((nb, L, M_QK), lambda b, c: (b, c, 0)),
            pl.BlockSpec((nb, L, M_QK), lambda b, c: (b, c, 0)),
            pl.BlockSpec((nb, L, M_V), lambda b, c: (b, c, 0)),
            pl.BlockSpec((nb, L, M_V), lambda b, c: (b, c, 0)),
            pl.BlockSpec((nb, 2 * H, L), lambda b, c: (b, 0, c)),
            pl.BlockSpec((nb, L, _NARROW_W), lambda b, c: (b, c, 0)),
            pl.BlockSpec((H, LANES), lambda b, c: (0, 0)),
            pl.BlockSpec((1, _NARROW_W), lambda b, c: (0, 0)),
            pl.BlockSpec((1, M_V), lambda b, c: (0, 0)),
        ],
        out_specs=pl.BlockSpec((nb, L, M_V), lambda b, c: (b, c, 0)),
        out_shape=jax.ShapeDtypeStruct((B, S, M_V), BF16),
        scratch_shapes=[pltpu.VMEM((nb * H, DK, DV + LANES), F32), pltpu.VMEM((nb * H, 1, 1), F32)],
        compiler_params=_cparams(("parallel", "arbitrary")),
        name="mlstm",
    )(q, k, v, o, narrow_t, narrow, fb, fb_lanes, norm_g[None, :])


def _compress_kernel(x_ref, pos_ref, w1_ref, w2_ref, out_ref, out_t_ref, x_scr):
    S = x_ref.shape[0]
    n = S // CMP_STRIDE
    x_scr[0:S] = x_ref[...]
    x_scr[S:] = jnp.zeros((CMP_STRIDE, LANES), F32)
    acc = jnp.zeros((n, w1_ref.shape[2]), F32)
    for l in range(CMP_BLOCK):
        tok = x_scr[pl.ds(l, n, stride=CMP_STRIDE), :] + pos_ref[l:l + 1, :]
        acc = acc + _dot(tok.astype(BF16), w1_ref[l])
    hid = acc * _sigmoid(acc)
    out = _dot(hid.astype(BF16), w2_ref[...])
    row = lax.broadcasted_iota(jnp.int32, out.shape, 0)
    out = jnp.where(row < n - 1, out, 0.0)
    out_ref[...] = out.astype(out_ref.dtype)
    out_t_ref[...] = out.T.astype(out_t_ref.dtype)


def _block_diag2(w):
    z = jnp.zeros_like(w)
    return jnp.concatenate([jnp.concatenate([w, z], axis=-1), jnp.concatenate([z, w], axis=-1)], axis=-2)


def _compress(cmp_kv, pos, w1, w2):
    B, S, _ = cmp_kv.shape
    n = S // CMP_STRIDE
    pos2 = jnp.concatenate([pos, pos], axis=-1)
    w1b = _block_diag2(w1.reshape(2, CMP_BLOCK, NSA_DH, CMP_HIDDEN)).astype(BF16)
    w2b = _block_diag2(w2).astype(BF16)
    return pl.pallas_call(
        _compress_kernel,
        grid=(B, 2),
        in_specs=[
            pl.BlockSpec((None, S, LANES), lambda b, a: (b, 0, a)),
            pl.BlockSpec((None, CMP_BLOCK, LANES), lambda b, a: (a, 0, 0)),
            pl.BlockSpec((None, CMP_BLOCK, LANES, 2 * CMP_HIDDEN), lambda b, a: (a, 0, 0, 0)),
            pl.BlockSpec((None, 2 * CMP_HIDDEN, LANES), lambda b, a: (a, 0, 0)),
        ],
        out_specs=[pl.BlockSpec((None, None, n, LANES), lambda b, a: (b, a, 0, 0)),
                   pl.BlockSpec((None, None, LANES, n), lambda b, a: (b, a, 0, 0))],
        out_shape=[jax.ShapeDtypeStruct((B, 2, n, LANES), BF16),
                   jax.ShapeDtypeStruct((B, 2, LANES, n), BF16)],
        scratch_shapes=[pltpu.VMEM((S + CMP_STRIDE, LANES), F32)],
        compiler_params=_cparams(("parallel", "parallel")),
        name="nsa_compress",
    )(cmp_kv, pos2, w1b, w2b)


_CMP_PER_SLC = SLC_BLOCK // CMP_STRIDE
_FORCED = 3


def _group_rows(q_h, g):
    z = jnp.zeros_like(q_h)
    return jnp.concatenate([q_h, z] if g == 0 else [z, q_h], axis=0)


def _select_kernel(q_ref, kc_ref, vct_ref, qt_ref, ocmp_ref, mask_ref, ps_scr):
    tq = q_ref.shape[0]
    t0 = pl.program_id(1) * tq
    q_t = (q_ref[...].astype(F32) * (NSA_DH ** -0.5)).T.astype(BF16)
    qt_ref[...] = q_t
    need = (t0 + tq) // CMP_STRIDE
    for nk in range(LANES, kc_ref.shape[0] + 1, LANES):
        pl.when((need - 1) // LANES == nk // LANES - 1)(
            functools.partial(_select_body, q_t, kc_ref, vct_ref, ocmp_ref, mask_ref, ps_scr, t0, nk))


def _select_body(q_t, kc_ref, vct_ref, ocmp_ref, mask_ref, ps_scr, t0, ncp):
    tq = q_t.shape[1]
    n_slc = ncp // _CMP_PER_SLC
    t_row = t0 + lax.broadcasted_iota(jnp.int32, (1, tq), 1)
    c_col = lax.broadcasted_iota(jnp.int32, (ncp, 1), 0)
    visible = (c_col * CMP_STRIDE + (CMP_BLOCK - 1)) <= t_row
    any_visible = t_row >= CMP_BLOCK - 1

    blk = lax.broadcasted_iota(jnp.int32, (n_slc, tq), 0)
    cur = (t0 + lax.broadcasted_iota(jnp.int32, (n_slc, tq), 1)) // SLC_BLOCK
    forced = (blk == 0) | (blk == cur) | (blk == cur - 1)
    eligible = blk <= cur

    def scores(hd):
        return _dot(kc_ref[0:ncp, :], _group_rows(q_t[hd * NSA_DH:(hd + 1) * NSA_DH], hd // NSA_HPG))

    s_next = scores(0)
    for g in range(NSA_KV_GROUPS):
        vct = vct_ref[g * NSA_DH:(g + 1) * NSA_DH, 0:ncp]
        for hh in range(NSA_HPG):
            hd = g * NSA_HPG + hh
            s = jnp.where(visible, s_next, NEG_BIG)
            if hd + 1 < NSA_HEADS:
                s_next = scores(hd + 1)
            e = jnp.exp(s - jnp.max(s, axis=0, keepdims=True))
            inv = jnp.where(any_visible, 1.0 / jnp.sum(e, axis=0, keepdims=True), 0.0)
            p = e * inv
            ocmp_ref[hd * NSA_DH:(hd + 1) * NSA_DH, :] = _dot(vct, p.astype(BF16)).astype(ocmp_ref.dtype)
            for c in range(tq // LANES):
                if hh == 0:
                    ps_scr[c, 0:ncp, :] = p[:, c * LANES:(c + 1) * LANES]
                else:
                    ps_scr[c, 0:ncp, :] += p[:, c * LANES:(c + 1) * LANES]
        lanes = [jnp.concatenate([ps_scr[c, pl.ds(r, n_slc, stride=_CMP_PER_SLC), :]
                                  for c in range(tq // LANES)], axis=1) for r in range(_CMP_PER_SLC)]
        row = lax.broadcasted_iota(jnp.int32, (n_slc, tq), 0)
        before = jnp.where(row == 0, 0.0, pltpu.roll(lanes[3], 1, 0))
        imp = before + 2.0 * (lanes[0] + lanes[1] + lanes[2]) + lanes[3]
        score = jnp.where(forced, -jnp.inf, jnp.where(eligible, imp, -SEL_BIG))
        for _ in range(SLC_TOPN - _FORCED):
            best = jnp.max(score, axis=0, keepdims=True)
            first = jnp.min(jnp.where(score == best, blk, SLC_SLOTS), axis=0, keepdims=True)
            score = jnp.where(blk == first, -jnp.inf, score)
        mask_ref[g, 0:n_slc, :] = jnp.where(score == -jnp.inf, 0.0, -MASK_BIG).astype(mask_ref.dtype)
        if n_slc < SLC_SLOTS:
            mask_ref[g, n_slc:, :] = jnp.full((SLC_SLOTS - n_slc, tq), -MASK_BIG, mask_ref.dtype)


def _select(n_q, cmp, cmp_t, tq):
    B, S, _ = n_q.shape
    G = NSA_KV_GROUPS
    ncp = cmp.shape[2]
    return pl.pallas_call(
        _select_kernel,
        grid=(B, S // tq),
        in_specs=[
            pl.BlockSpec((None, tq, N_Q), lambda b, i: (b, i, 0)),
            pl.BlockSpec((None, None, ncp, LANES), lambda b, i: (b, 0, 0, 0)),
            pl.BlockSpec((None, None, LANES, ncp), lambda b, i: (b, 1, 0, 0)),
        ],
        out_specs=[
            pl.BlockSpec((None, N_Q, tq), lambda b, i: (b, 0, i)),
            pl.BlockSpec((None, N_Q, tq), lambda b, i: (b, 0, i)),
            pl.BlockSpec((None, G, SLC_SLOTS, tq), lambda b, i: (b, 0, 0, i)),
        ],
        out_shape=[
            jax.ShapeDtypeStruct((B, N_Q, S), BF16),
            jax.ShapeDtypeStruct((B, N_Q, S), BF16),
            jax.ShapeDtypeStruct((B, G, SLC_SLOTS, S), BF16),
        ],
        scratch_shapes=[pltpu.VMEM((tq // LANES, ncp, LANES), F32)],
        compiler_params=_cparams(("parallel", "parallel")),
        name="nsa_select",
    )(n_q, cmp, cmp_t)


_TK = 256
_KAUG_W = SLC_SLOTS + N_KV
_AHEAD = 7
_TILES_PER_TRIP = 8


def _pipelined(work, score, consume):
    pending = [score(item) for item in work[:_AHEAD]]
    for i, item in enumerate(work):
        s = pending.pop(0)
        if i + _AHEAD < len(work):
            pending.append(score(work[i + _AHEAD]))
        consume(item, s)


def _attend_kernel(qt_ref, mask_ref, ocmp_ref, gate_ref, kaug_ref, vt_ref, kw_ref, vwt_ref, y_ref,
                   qa_scr, m_scr, acc_scr, yt_scr):
    tq = qt_ref.shape[1]
    S = kaug_ref.shape[0]
    G = NSA_KV_GROUPS
    q0 = pl.program_id(1) * tq
    t_q = q0 + lax.broadcasted_iota(jnp.int32, (1, tq), 1)
    n_full = q0 // _TK
    kd = pl.multiple_of(n_full * _TK, _TK)
    causal = (kd + lax.broadcasted_iota(jnp.int32, (_TK, 1), 0)) <= t_q
    heads = [(g, hh) for g in range(G) for hh in range(NSA_HPG)]

    for g, hh in heads:
        hd = g * NSA_HPG + hh
        cols = slice(hh * tq, (hh + 1) * tq)
        qa_scr[g, 0:SLC_SLOTS, cols] = mask_ref[g]
        qa_scr[g, SLC_SLOTS:, cols] = _group_rows(qt_ref[hd * NSA_DH:(hd + 1) * NSA_DH, :], g)

    m_scr[...] = jnp.full(m_scr.shape, NEG_BIG, F32)
    acc_scr[...] = jnp.zeros_like(acc_scr)

    def selected_scores(item):
        (g, hh), k0 = item
        return _dot(kaug_ref[pl.ds(k0, _TK), :], qa_scr[g, :, hh * tq:(hh + 1) * tq])

    def selected_tiles(starts, diagonal):
        def consume(item, s):
            (g, hh), k0 = item
            cols = slice(hh * tq, (hh + 1) * tq)
            if diagonal:
                s = jnp.where(causal, s, -MASK_BIG)
            m_old = m_scr[g, :, cols]
            m_new = jnp.maximum(m_old, jnp.max(s, axis=0, keepdims=True).astype(BF16).astype(F32))
            p = jnp.exp(s.astype(BF16) - m_new[0:1].astype(BF16))
            acc_scr[g, :, cols] = (jnp.exp(m_old - m_new)[0:1] * acc_scr[g, :, cols]
                                   + _dot(vt_ref[g, :, pl.ds(k0, _TK)], p))
            m_scr[g, :, cols] = m_new
        _pipelined([(head, k0) for k0 in starts for head in heads], selected_scores, consume)

    def run_tiles(first, count):
        selected_tiles([pl.multiple_of(first + n * _TK, _TK) for n in range(count)], False)

    def stretch(j, carry):
        run_tiles(j * (_TILES_PER_TRIP * _TK), _TILES_PER_TRIP)
        return carry

    lax.fori_loop(0, n_full // _TILES_PER_TRIP, stretch, 0)
    count = _TILES_PER_TRIP // 2
    while count:
        pl.when(n_full & count != 0)(functools.partial(
            run_tiles, (n_full - n_full % (2 * count)) * _TK, count))
        count //= 2
    selected_tiles([kd], True)

    gates_t = _sigmoid(gate_ref[...]).T
    wlen = min(tq + WINDOW, S)
    w0 = pl.multiple_of(jnp.maximum(q0 - WINDOW, 0), tq)

    n_wt = wlen // _TK
    w_tile = [pl.multiple_of(w0 + j * _TK, _TK) for j in range(n_wt)]

    def window_scores(g, hh):
        q_cols = qa_scr[g, SLC_SLOTS:, hh * tq:(hh + 1) * tq]
        return [_dot(kw_ref[pl.ds(w_tile[j], _TK), :], q_cols) for j in range(n_wt)]

    def window(band_mask):
        s_next = window_scores(*heads[0])
        for i, (g, hh) in enumerate(heads):
            s = band_mask(jnp.concatenate(s_next, axis=0))
            if i + 1 < len(heads):
                s_next = window_scores(*heads[i + 1])
            hd = g * NSA_HPG + hh
            rows_h = slice(hd * NSA_DH, (hd + 1) * NSA_DH)
            e = jnp.exp(s.astype(BF16) - jnp.max(s, axis=0, keepdims=True).astype(BF16))
            ow = sum(_dot(vwt_ref[g, :, pl.ds(w_tile[j], _TK)], e[j * _TK:(j + 1) * _TK])
                     for j in range(n_wt))
            acc = acc_scr[g, :, hh * tq:(hh + 1) * tq]
            gr = GATE_COL0 + 3 * hd
            yt_scr[rows_h, :] = (gates_t[gr:gr + 1] * ocmp_ref[rows_h, :].astype(F32)
                                 + gates_t[gr + 1:gr + 2] * (acc[:NSA_DH] / acc[NSA_DH:NSA_DH + 1])
                                 + gates_t[gr + 2:gr + 3] * (ow[:NSA_DH] / ow[NSA_DH:NSA_DH + 1]))

    def general_band(s):
        kpos = w0 + lax.broadcasted_iota(jnp.int32, (wlen, 1), 0)
        return jnp.where((kpos <= t_q) & (kpos > t_q - WINDOW), s, NEG_BIG)

    if wlen == tq + WINDOW and WINDOW % tq == 0:
        r = lax.broadcasted_iota(jnp.int32, (tq, tq), 0)
        c = lax.broadcasted_iota(jnp.int32, (tq, tq), 1)

        def interior_band(s):
            return jnp.concatenate([jnp.where(r > c, s[:tq], NEG_BIG), s[tq:WINDOW],
                                    jnp.where(r <= c, s[WINDOW:], NEG_BIG)], axis=0)

        pl.when(q0 >= WINDOW)(functools.partial(window, interior_band))
        pl.when(q0 < WINDOW)(functools.partial(window, general_band))
    else:
        window(general_band)
    y_ref[...] = yt_scr[...].T.astype(y_ref.dtype)


def _attend(q_t, mask_t, ocmp_t, narrow, kaug, v_t, kw, vw_t, tq):
    B, _, S = q_t.shape
    G = NSA_KV_GROUPS
    rows = NSA_HPG * tq
    return pl.pallas_call(
        _attend_kernel,
        grid=(B, S // tq),
        in_specs=[
            pl.BlockSpec((None, N_Q, tq), lambda b, i: (b, 0, i)),
            pl.BlockSpec((None, G, SLC_SLOTS, tq), lambda b, i: (b, 0, 0, i)),
            pl.BlockSpec((None, N_Q, tq), lambda b, i: (b, 0, i)),
            pl.BlockSpec((None, tq, _NARROW_W), lambda b, i: (b, i, 0)),
            pl.BlockSpec((None, S, _KAUG_W), lambda b, i: (b, 0, 0)),
            pl.BlockSpec((None, G, _VT_ROWS, S), lambda b, i: (b, 0, 0, 0)),
            pl.BlockSpec((None, S, N_KV), lambda b, i: (b, 0, 0)),
            pl.BlockSpec((None, G, _VT_ROWS, S), lambda b, i: (b, 0, 0, 0)),
        ],
        out_specs=pl.BlockSpec((None, tq, N_Q), lambda b, i: (b, i, 0)),
        out_shape=jax.ShapeDtypeStruct((B, S, N_Q), BF16),
        scratch_shapes=[
            pltpu.VMEM((G, _KAUG_W, rows), BF16),
            pltpu.VMEM((G, 8, rows), F32),
            pltpu.VMEM((G, _VT_ROWS, rows), F32),
            pltpu.VMEM((N_Q, tq), F32),
        ],
        compiler_params=_cparams(("parallel", "arbitrary")),
        name="nsa_attend",
    )(q_t, mask_t, ocmp_t, narrow, kaug, v_t, kw, vw_t)


def _merge_kernel(x_ref, ya_ref, yb_ref, ga_ref, gb_ref, wa_ref, wb_ref, wo_ref, g2_ref, x1_ref, h2_ref):
    a = _sigmoid(ga_ref[...].astype(F32)) * _dot(ya_ref[...], wa_ref[...])
    b = _sigmoid(gb_ref[...].astype(F32)) * _dot(yb_ref[...], wb_ref[...])
    x1 = x_ref[...] + _dot((a + b).astype(BF16), wo_ref[...])
    x1_ref[...] = x1
    h2_ref[...] = _rms(x1, g2_ref[...]).astype(h2_ref.dtype)


def _merge(x2d, ya, yb, ga, gb, wa, wb, wo, g2, tm):
    T = x2d.shape[0]
    row = lambda w: pl.BlockSpec((tm, w), lambda i: (i, 0))
    const = lambda r, c: pl.BlockSpec((r, c), lambda i: (0, 0))
    return pl.pallas_call(
        _merge_kernel,
        grid=(T // tm,),
        in_specs=[row(D_MODEL), row(M_V), row(N_Q), row(D_MODEL), row(D_MODEL),
                  const(M_V, D_MODEL), const(N_Q, D_MODEL), const(D_MODEL, D_MODEL), const(1, D_MODEL)],
        out_specs=[row(D_MODEL), row(D_MODEL)],
        out_shape=[jax.ShapeDtypeStruct((T, D_MODEL), F32), jax.ShapeDtypeStruct((T, D_MODEL), BF16)],
        compiler_params=_cparams(("parallel",)),
        name="merge",
    )(x2d, ya, yb, ga, gb, wa.astype(BF16), wb.astype(BF16), wo.astype(BF16), g2[None, :])


_FFN_CHUNK = 256


def _ffn_kernel(x1_ref, h2_ref, wg_ref, wu_ref, wd_ref, gf_ref, out_ref):
    h = h2_ref[...]
    acc = x1_ref[...]
    for c in range(FFN_HIDDEN // _FFN_CHUNK):
        sl = slice(c * _FFN_CHUNK, (c + 1) * _FFN_CHUNK)
        gate = _dot(h, wg_ref[:, sl])
        up = _dot(h, wu_ref[:, sl])
        acc = acc + _dot((gate * _sigmoid(gate) * up).astype(BF16), wd_ref[sl, :])
    out_ref[...] = _rms(acc, gf_ref[...])


def _ffn(x1, h2, wg, wu, wd, gf, tm):
    T = x1.shape[0]
    row = lambda: pl.BlockSpec((tm, D_MODEL), lambda i: (i, 0))
    const = lambda r, c: pl.BlockSpec((r, c), lambda i: (0, 0))
    return pl.pallas_call(
        _ffn_kernel,
        grid=(T // tm,),
        in_specs=[row(), row(), const(D_MODEL, FFN_HIDDEN), const(D_MODEL, FFN_HIDDEN),
                  const(FFN_HIDDEN, D_MODEL), const(1, D_MODEL)],
        out_specs=row(),
        out_shape=jax.ShapeDtypeStruct((T, D_MODEL), F32),
        compiler_params=_cparams(("parallel",)),
        name="ffn",
    )(x1, h2, wg.astype(BF16), wu.astype(BF16), wd.astype(BF16), gf[None, :])


def _layer(x, norm1_g, w_in, b_in, f_bias, conv_w, conv_b, mlstm_norm_g, cmp_k_pos, cmp_k_w1, cmp_k_w2,
           cmp_v_pos, cmp_v_w1, cmp_v_w2, w_branch_a, w_branch_b, w_out, norm2_g):
    B, S, D = x.shape
    T = B * S
    tm = min(256, S)

    (m_q, m_k, m_v, m_o, n_q, gate_a, gate_b, narrow, narrow_t, cmp_kv, kaug, vs_t, k_win, vw_t) = _inproj(
        x, norm1_g, w_in, b_in, conv_w, conv_b, tm)
    narrow = narrow.reshape(B, S, _NARROW_W)

    y_a = _mlstm(m_q.reshape(B, S, M_QK), m_k.reshape(B, S, M_QK), m_v.reshape(B, S, M_V),
                 m_o.reshape(B, S, M_V), narrow, narrow_t, f_bias, mlstm_norm_g, min(512, S))

    cmp, cmp_t = _compress(cmp_kv.reshape(B, S, 2 * N_KV), jnp.stack([cmp_k_pos, cmp_v_pos]),
                           jnp.stack([cmp_k_w1, cmp_v_w1]), jnp.stack([cmp_k_w2, cmp_v_w2]))
    tq = min(256, S)
    q_t, ocmp_t, mask_t = _select(n_q.reshape(B, S, N_Q), cmp, cmp_t, tq)
    y_b = _attend(q_t, mask_t, ocmp_t, narrow, kaug.reshape(B, S, _KAUG_W), vs_t,
                  k_win.reshape(B, S, N_KV), vw_t, tq)

    return _merge(x.reshape(T, D), y_a.reshape(T, M_V), y_b.reshape(T, N_Q), gate_a, gate_b,
                  w_branch_a, w_branch_b, w_out, norm2_g, min(512, T))


def kernel(x, norm1_g, w_in, b_in, f_bias, conv_w, conv_b, mlstm_norm_g, cmp_k_pos, cmp_k_w1, cmp_k_w2,
           cmp_v_pos, cmp_v_w1, cmp_v_w2, w_branch_a, w_branch_b, w_out, norm2_g, w_ffn_gate, w_ffn_up,
           w_ffn_down, norm_f_g):
    B, S, D = x.shape
    depth = w_in.shape[0]
    assert depth == 1, "the fused final norm assumes a single layer"
    x1, h2 = _layer(x, norm1_g[0], w_in[0], b_in[0], f_bias[0], conv_w[0], conv_b[0], mlstm_norm_g[0],
                    cmp_k_pos[0], cmp_k_w1[0], cmp_k_w2[0], cmp_v_pos[0], cmp_v_w1[0], cmp_v_w2[0],
                    w_branch_a[0], w_branch_b[0], w_out[0], norm2_g[0])
    out = _ffn(x1, h2, w_ffn_gate[0], w_ffn_up[0], w_ffn_down[0], norm_f_g, min(512, B * S))
    return out.reshape(B, S, D)
```

```python
import functools

import numpy as np
import jax
import jax.numpy as jnp
from jax import lax
from jax.experimental import pallas as pl
from jax.experimental.pallas import tpu as pltpu

F32 = jnp.float32
BF16 = jnp.bfloat16

D_MODEL = 1024
MLSTM_HEADS = 4
MLSTM_DV = D_MODEL // MLSTM_HEADS
MLSTM_DQK = MLSTM_DV // 2
CONV_K = 4
NSA_DH = 64
NSA_HEADS = (D_MODEL // 2) // NSA_DH
NSA_KV_GROUPS = 2
NSA_HPG = NSA_HEADS // NSA_KV_GROUPS
CMP_BLOCK = 32
CMP_STRIDE = 16
CMP_HIDDEN = 256
SLC_BLOCK = 64
SLC_TOPN = 16
WINDOW = 512
FFN_HIDDEN = 2816
RMS_EPS = 1e-6
SEL_BIG = 1e9

M_QK = MLSTM_HEADS * MLSTM_DQK
M_V = MLSTM_HEADS * MLSTM_DV
N_Q = NSA_HEADS * NSA_DH
N_KV = NSA_KV_GROUPS * NSA_DH
IN_SPLITS = (2 * M_QK, M_V, M_V, MLSTM_HEADS, MLSTM_HEADS, N_Q, 6 * N_KV, 3 * NSA_HEADS, D_MODEL, D_MODEL)

LANES = 128
SLC_SLOTS = 128
MASK_BIG = float(2.0 ** 100)
NEG_BIG = -1e30
VMEM_LIMIT = 56 * 1024 * 1024
assert N_KV == LANES, "both KV groups are packed into one 128-lane key row"

_SEG = dict(zip(("m_qk", "m_v", "m_o", "m_i", "m_f", "n_q", "n_kv", "n_g", "gate_a", "gate_b"),
                zip(np.cumsum((0,) + IN_SPLITS[:-1]).tolist(), IN_SPLITS)))
_WIDE = ("m_qk", "m_v", "m_o", "n_q", "n_kv", "gate_a", "gate_b")
_NARROW = ("m_i", "m_f", "n_g")
_NARROW_W = LANES
GATE_COL0 = 2 * MLSTM_HEADS
_VT_ROWS = 80


def _cparams(sem):
    return pltpu.CompilerParams(dimension_semantics=sem, vmem_limit_bytes=VMEM_LIMIT)


def _sigmoid(x):
    return 1.0 / (1.0 + jnp.exp(-x))


def _log_sigmoid(x):
    return jnp.minimum(x, 0.0) - jnp.log(1.0 + jnp.exp(-jnp.abs(x)))


def _rms(x, g):
    ms = jnp.mean(x * x, axis=-1, keepdims=True)
    return x * lax.rsqrt(ms + RMS_EPS) * g


def _dot_nt(a, b):
    return lax.dot_general(a, b, (((1,), (1,)), ((), ())), preferred_element_type=F32)


def _dot(a, b):
    return jnp.dot(a, b, preferred_element_type=F32)


def _values_t(v):
    v_t = v.T
    row = lax.broadcasted_iota(jnp.int32, (_VT_ROWS - NSA_DH, v.shape[0]), 0)
    tail = jnp.where(row == 0, 1.0, 0.0)
    return [jnp.concatenate([v_t[g * NSA_DH:(g + 1) * NSA_DH], tail], axis=0).astype(BF16)
            for g in range(NSA_KV_GROUPS)]


_HALO = 8
_CONV_CHUNK = 2 * M_QK // 4
assert M_QK % _CONV_CHUNK == 0
_CONV_ROWS = 256


def _inproj_kernel(x_ref, g_ref, w_ref, b_ref, cw_ref, cb_ref, q_ref, k_ref, v_ref, o_ref, nq_ref, ga_ref,
                   gb_ref, narrow_ref, narrow_t_ref, cmp_ref, kaug_ref, vst_ref, kw_ref, vwt_ref, halo_scr,
                   *, offs, tiles_per_seq):
    tm = x_ref.shape[0]
    first_of_seq = pl.program_id(0) % tiles_per_seq == 0
    h = _rms(x_ref[...], g_ref[...]).astype(BF16)

    def seg(off, width):
        return _dot(h, w_ref[:, off:off + width]) + b_ref[:, off:off + width]

    def conv_chunk(c):
        cols = slice(c * _CONV_CHUNK, (c + 1) * _CONV_CHUNK)
        qk = seg(offs["m_qk"] + c * _CONV_CHUNK, _CONV_CHUNK)
        halo = jnp.where(first_of_seq, 0.0, halo_scr[:, cols])
        halo_scr[:, cols] = qk[tm - _HALO:]
        for r0 in range(0, tm, _CONV_ROWS):
            rows = slice(r0, r0 + _CONV_ROWS)
            piece = qk[rows]
            padded = jnp.concatenate([halo if r0 == 0 else qk[r0 - _HALO:r0], piece], axis=0)
            acc = piece * cw_ref[CONV_K - 1:CONV_K, cols] + cb_ref[:, cols]
            for j in range(1, CONV_K):
                acc = acc + pltpu.roll(padded, j, 0)[_HALO:] * cw_ref[CONV_K - 1 - j:CONV_K - j, cols]
            act = acc * _sigmoid(acc)
            if c * _CONV_CHUNK < M_QK:
                q_ref[rows, cols] = act.astype(BF16)
            else:
                k_ref[rows, c * _CONV_CHUNK - M_QK:(c + 1) * _CONV_CHUNK - M_QK] = (
                    act * (MLSTM_DQK ** -0.5)).astype(BF16)

    others = [(v_ref, offs["m_v"], M_V), (o_ref, offs["m_o"], M_V), (ga_ref, offs["gate_a"], D_MODEL),
              (gb_ref, offs["gate_b"], D_MODEL)]
    per = 2 * M_QK // _CONV_CHUNK // len(others)
    for n, (ref, off, width) in enumerate(others):
        for c in range(n * per, (n + 1) * per):
            conv_chunk(c)
        ref[...] = seg(off, width).astype(BF16)
    nq_ref[...] = seg(offs["n_q"], N_Q).astype(BF16)
    narrow = seg(offs["narrow"], _NARROW_W)
    narrow_ref[...] = narrow
    narrow_t_ref[...] = narrow.T

    kv0 = offs["n_kv"]
    cmp_ref[...] = seg(kv0, 2 * N_KV)
    pos = (pl.program_id(0) % tiles_per_seq) * tm + lax.broadcasted_iota(jnp.int32, (tm, SLC_SLOTS), 0)
    onehot = (pos // SLC_BLOCK == lax.broadcasted_iota(jnp.int32, (tm, SLC_SLOTS), 1)).astype(BF16)
    kaug_ref[...] = jnp.concatenate([onehot, seg(kv0 + 2 * N_KV, N_KV).astype(BF16)], axis=1)
    for g, vt in enumerate(_values_t(seg(kv0 + 3 * N_KV, N_KV))):
        vst_ref[g] = vt
    kw_ref[...] = seg(kv0 + 4 * N_KV, N_KV).astype(BF16)
    for g, vt in enumerate(_values_t(seg(kv0 + 5 * N_KV, N_KV))):
        vwt_ref[g] = vt


def _inproj(x, norm_g, w_in, b_in, conv_w, conv_b, tm):
    B, S, _ = x.shape
    T = B * S
    G = NSA_KV_GROUPS
    cols = [w_in[:, _SEG[n][0]:_SEG[n][0] + _SEG[n][1]] for n in _WIDE]
    bias = [b_in[_SEG[n][0]:_SEG[n][0] + _SEG[n][1]] for n in _WIDE]
    narrow_w = jnp.concatenate([w_in[:, _SEG[n][0]:_SEG[n][0] + _SEG[n][1]] for n in _NARROW], axis=1)
    narrow_b = jnp.concatenate([b_in[_SEG[n][0]:_SEG[n][0] + _SEG[n][1]] for n in _NARROW])
    pad = _NARROW_W - narrow_w.shape[1]
    cols.append(jnp.pad(narrow_w, ((0, 0), (0, pad))))
    bias.append(jnp.pad(narrow_b, (0, pad)))
    w = jnp.concatenate(cols, axis=1).astype(BF16)
    b = jnp.concatenate(bias)[None, :].astype(F32)
    widths = [_SEG[n][1] for n in _WIDE] + [_NARROW_W]
    offs = dict(zip(_WIDE + ("narrow",), np.cumsum([0] + widths[:-1]).tolist()))
    n_tot = w.shape[1]
    tps = S // tm

    def rows(width, dtype):
        return pl.BlockSpec((tm, width), lambda i: (i, 0)), jax.ShapeDtypeStruct((T, width), dtype)

    def rows_t(lead, dtype):
        nd = len(lead)
        return (pl.BlockSpec((None,) + lead + (tm,), lambda i: (i // tps,) + (0,) * nd + (i % tps,)),
                jax.ShapeDtypeStruct((B,) + lead + (S,), dtype))

    outs = [rows(M_QK, BF16), rows(M_QK, BF16), rows(M_V, BF16), rows(M_V, BF16), rows(N_Q, BF16),
            rows(D_MODEL, BF16), rows(D_MODEL, BF16), rows(_NARROW_W, F32), rows_t((_NARROW_W,), F32),
            rows(2 * N_KV, F32), rows(SLC_SLOTS + N_KV, BF16), rows_t((G, _VT_ROWS), BF16), rows(N_KV, BF16),
            rows_t((G, _VT_ROWS), BF16)]
    return pl.pallas_call(
        functools.partial(_inproj_kernel, offs=offs, tiles_per_seq=tps),
        grid=(T // tm,),
        in_specs=[
            pl.BlockSpec((tm, D_MODEL), lambda i: (i, 0)),
            pl.BlockSpec((1, D_MODEL), lambda i: (0, 0)),
            pl.BlockSpec((D_MODEL, n_tot), lambda i: (0, 0)),
            pl.BlockSpec((1, n_tot), lambda i: (0, 0)),
            pl.BlockSpec((CONV_K, 2 * M_QK), lambda i: (0, 0)),
            pl.BlockSpec((1, 2 * M_QK), lambda i: (0, 0)),
        ],
        out_specs=[o[0] for o in outs],
        out_shape=[o[1] for o in outs],
        scratch_shapes=[pltpu.VMEM((_HALO, 2 * M_QK), F32)],
        compiler_params=_cparams(("arbitrary",)),
        name="inproj",
    )(x.reshape(T, D_MODEL), norm_g[None, :], w, b, conv_w, conv_b[None, :])


def _mlstm_kernel(q_ref, k_ref, v_ref, o_ref, gr_ref, gc_ref, fb_ref, fbl_ref, ng_ref, y_ref, c_scr, m_scr):
    H, DK, DV = MLSTM_HEADS, MLSTM_DQK, MLSTM_DV
    L = q_ref.shape[1]
    chains = [(bi, h) for bi in range(q_ref.shape[0]) for h in range(H)]
    heads = range(len(chains))

    @pl.when(pl.program_id(1) == 0)
    def _():
        c_scr[...] = jnp.zeros_like(c_scr)
        m_scr[...] = jnp.zeros_like(m_scr)

    r_t = lax.broadcasted_iota(jnp.int32, (L, L), 0)
    r_s = lax.broadcasted_iota(jnp.int32, (L, L), 1)
    causal = r_s <= r_t
    upper = r_t <= r_s
    gates = [gc_ref[bi] for bi in range(q_ref.shape[0])]
    lf_cols = [_log_sigmoid(g + fbl_ref[...]) for g in gates]

    qs = [q_ref[bi, :, h * DK:(h + 1) * DK] for bi, h in chains]
    ks = [k_ref[bi, :, h * DK:(h + 1) * DK] for bi, h in chains]
    qk = [_dot_nt(qs[n], ks[n]) for n in heads]
    v_ext = [jnp.concatenate([v_ref[bi, :, h * DV:(h + 1) * DV], jnp.ones((L, LANES), BF16)], axis=1)
             for bi, h in chains]
    state = [c_scr[n] for n in heads]
    inter = [_dot(qs[n], state[n].astype(BF16)) for n in heads]

    stab = []
    for n, (bi, h) in enumerate(chains):
        fb = fb_ref[h:h + 1, 0:1]
        i_row = gr_ref[bi, h:h + 1, :]
        lf_row = _log_sigmoid(gr_ref[bi, H + h:H + h + 1, :] + fb)
        i_col = gates[bi][:, h:h + 1]
        lf_col = lf_cols[bi][:, H + h:H + h + 1]
        m_prev = m_scr[n]
        b_col = jnp.sum(jnp.where(causal, lf_row, 0.0), axis=1, keepdims=True)
        b_row = jnp.sum(jnp.where(upper, lf_col, 0.0), axis=0, keepdims=True)
        b_last = jnp.sum(lf_row, axis=1, keepdims=True)
        d_intra = jnp.where(causal, b_col - b_row + i_row, NEG_BIG)
        d_inter = b_col + m_prev
        m_t = jnp.maximum(d_inter, jnp.max(d_intra, axis=1, keepdims=True))
        d_state_row = b_last - b_row + i_row
        m_new = jnp.maximum(b_last + m_prev, jnp.max(d_state_row, axis=1, keepdims=True))
        stab.append(dict(w_intra=jnp.exp(d_intra - m_t), w_inter=jnp.exp(d_inter - m_t), floor=jnp.exp(-m_t),
                         w_state=jnp.exp(b_last - b_col + i_col - m_new),
                         decay=jnp.exp(b_last + m_prev - m_new), m_new=m_new))

    s = [(qk[n] * stab[n]["w_intra"]).astype(BF16) for n in heads]
    num = [_dot(s[n], v_ext[n]) + stab[n]["w_inter"] * inter[n] for n in heads]
    for n, (bi, h) in enumerate(chains):
        cols = slice(h * DV, (h + 1) * DV)
        hid = num[n][:, :DV] / jnp.maximum(jnp.abs(num[n][:, DV:DV + 1]), stab[n]["floor"])
        hid = _rms(hid, ng_ref[:, cols])
        y_ref[bi, :, cols] = (_sigmoid(o_ref[bi, :, cols].astype(F32)) * hid).astype(y_ref.dtype)
    k_t = [ks[n].astype(F32).T.astype(BF16) for n in heads]
    vw = [v_ext[n] * stab[n]["w_state"].astype(BF16) for n in heads]
    for n in heads:
        c_scr[n] = stab[n]["decay"] * state[n] + _dot(k_t[n], vw[n])
        m_scr[n] = stab[n]["m_new"]


def _mlstm(q, k, v, o, narrow, narrow_t, f_bias, norm_g, L):
    B, S, _ = q.shape
    H, DK, DV = MLSTM_HEADS, MLSTM_DQK, MLSTM_DV
    fb = jnp.broadcast_to(f_bias.astype(F32)[:, None], (H, LANES))
    fb_lanes = jnp.zeros((1, _NARROW_W), F32).at[0, H:2 * H].set(f_bias.astype(F32))
    nb = 2 if B % 2 == 0 else 1
    return pl.pallas_call(
        _mlstm_kernel,
        grid=(B // nb, S // L),
        in_specs=[
            pl.BlockSpec((nb, L, M_QK), lambda b, c: (b, c, 0)),
            pl.BlockSpec((nb, L, M_QK), lambda b, c: (b, c, 0)),
            pl.BlockSpec((nb, L, M_V), lambda b, c: (b, c, 0)),
            pl.BlockSpec((nb, L, M_V), lambda b, c: (b, c, 0)),
            pl.BlockSpec((nb, 2 * H, L), lambda b, c: (b, 0, c)),
            pl.BlockSpec((nb, L, _NARROW_W), lambda b, c: (b, c, 0)),
            pl.BlockSpec((H, LANES), lambda b, c: (0, 0)),
            pl.BlockSpec((1, _NARROW_W), lambda b, c: (0, 0)),
            pl.BlockSpec((1, M_V), lambda b, c: (0, 0)),
        ],
        out_specs=pl.BlockSpec((nb, L, M_V), lambda b, c: (b, c, 0)),
        out_shape=jax.ShapeDtypeStruct((B, S, M_V), BF16),
        scratch_shapes=[pltpu.VMEM((nb * H, DK, DV + LANES), F32), pltpu.VMEM((nb * H, 1, 1), F32)],
        compiler_params=_cparams(("parallel", "arbitrary")),
        name="mlstm",
    )(q, k, v, o, narrow_t, narrow, fb, fb_lanes, norm_g[None, :])


def _compress_kernel(x_ref, pos_ref, w1_ref, w2_ref, out_ref, out_t_ref, x_scr):
    S = x_ref.shape[0]
    n = S // CMP_STRIDE
    x_scr[0:S] = x_ref[...]
    x_scr[S:] = jnp.zeros((CMP_STRIDE, LANES), F32)
    acc = jnp.zeros((n, w1_ref.shape[2]), F32)
    for l in range(CMP_BLOCK):
        tok = x_scr[pl.ds(l, n, stride=CMP_STRIDE), :] + pos_ref[l:l + 1, :]
        acc = acc + _dot(tok.astype(BF16), w1_ref[l])
    hid = acc * _sigmoid(acc)
    out = _dot(hid.astype(BF16), w2_ref[...])
    row = lax.broadcasted_iota(jnp.int32, out.shape, 0)
    out = jnp.where(row < n - 1, out, 0.0)
    out_ref[...] = out.astype(out_ref.dtype)
    out_t_ref[...] = out.T.astype(out_t_ref.dtype)


def _block_diag2(w):
    z = jnp.zeros_like(w)
    return jnp.concatenate([jnp.concatenate([w, z], axis=-1), jnp.concatenate([z, w], axis=-1)], axis=-2)


def _compress(cmp_kv, pos, w1, w2):
    B, S, _ = cmp_kv.shape
    n = S // CMP_STRIDE
    pos2 = jnp.concatenate([pos, pos], axis=-1)
    w1b = _block_diag2(w1.reshape(2, CMP_BLOCK, NSA_DH, CMP_HIDDEN)).astype(BF16)
    w2b = _block_diag2(w2).astype(BF16)
    return pl.pallas_call(
        _compress_kernel,
        grid=(B, 2),
        in_specs=[
            pl.BlockSpec((None, S, LANES), lambda b, a: (b, 0, a)),
            pl.BlockSpec((None, CMP_BLOCK, LANES), lambda b, a: (a, 0, 0)),
            pl.BlockSpec((None, CMP_BLOCK, LANES, 2 * CMP_HIDDEN), lambda b, a: (a, 0, 0, 0)),
            pl.BlockSpec((None, 2 * CMP_HIDDEN, LANES), lambda b, a: (a, 0, 0)),
        ],
        out_specs=[pl.BlockSpec((None, None, n, LANES), lambda b, a: (b, a, 0, 0)),
                   pl.BlockSpec((None, None, LANES, n), lambda b, a: (b, a, 0, 0))],
        out_shape=[jax.ShapeDtypeStruct((B, 2, n, LANES), BF16),
                   jax.ShapeDtypeStruct((B, 2, LANES, n), BF16)],
        scratch_shapes=[pltpu.VMEM((S + CMP_STRIDE, LANES), F32)],
        compiler_params=_cparams(("parallel", "parallel")),
        name="nsa_compress",
    )(cmp_kv, pos2, w1b, w2b)


_CMP_PER_SLC = SLC_BLOCK // CMP_STRIDE
_FORCED = 3
_LOG2E = 1.4426950408889634


def _group_rows(q_h, g):
    z = jnp.zeros_like(q_h)
    return jnp.concatenate([q_h, z] if g == 0 else [z, q_h], axis=0)


def _select_kernel(q_ref, kc_ref, vct_ref, qt_ref, ocmp_ref, mask_ref, ps_scr):
    tq = q_ref.shape[0]
    t0 = pl.program_id(1) * tq
    q_f = q_ref[...].astype(F32).T * (NSA_DH ** -0.5)
    qt_ref[...] = q_f.astype(BF16)
    q_t = (q_f * _LOG2E).astype(BF16)
    need = (t0 + tq) // CMP_STRIDE
    for nk in range(LANES, kc_ref.shape[0] + 1, LANES):
        pl.when((need - 1) // LANES == nk // LANES - 1)(
            functools.partial(_select_body, q_t, kc_ref, vct_ref, ocmp_ref, mask_ref, ps_scr, t0, nk))


def _select_body(q_t, kc_ref, vct_ref, ocmp_ref, mask_ref, ps_scr, t0, ncp):
    tq = q_t.shape[1]
    n_slc = ncp // _CMP_PER_SLC
    t_row = t0 + lax.broadcasted_iota(jnp.int32, (1, tq), 1)
    n_vis = max(ncp - LANES - tq // CMP_STRIDE - 2 * CMP_BLOCK // CMP_STRIDE, 0) // 8 * 8
    c_col = n_vis + lax.broadcasted_iota(jnp.int32, (ncp - n_vis, 1), 0)
    visible = (c_col * CMP_STRIDE + (CMP_BLOCK - 1)) <= t_row
    any_visible = t_row >= CMP_BLOCK - 1

    blk = lax.broadcasted_iota(jnp.int32, (n_slc, tq), 0)
    cur = (t0 + lax.broadcasted_iota(jnp.int32, (n_slc, tq), 1)) // SLC_BLOCK
    forced = (blk == 0) | (blk == cur) | (blk == cur - 1)
    eligible = blk <= cur

    def scores(hd):
        return _dot(kc_ref[0:ncp, :], _group_rows(q_t[hd * NSA_DH:(hd + 1) * NSA_DH], hd // NSA_HPG))

    s_next = scores(0)
    for g in range(NSA_KV_GROUPS):
        vct = vct_ref[g * NSA_DH:(g + 1) * NSA_DH, 0:ncp]
        for hh in range(NSA_HPG):
            hd = g * NSA_HPG + hh
            s = jnp.where(visible, s_next[n_vis:], NEG_BIG)
            if n_vis:
                s = jnp.concatenate([s_next[:n_vis], s], axis=0)
            if hd + 1 < NSA_HEADS:
                s_next = scores(hd + 1)
            e = jnp.exp2(s - jnp.max(s, axis=0, keepdims=True))
            inv = jnp.where(any_visible, 1.0 / jnp.sum(e, axis=0, keepdims=True), 0.0)
            p = e * inv
            ocmp_ref[hd * NSA_DH:(hd + 1) * NSA_DH, :] = _dot(vct, p.astype(BF16)).astype(ocmp_ref.dtype)
            for c in range(tq // LANES):
                if hh == 0:
                    ps_scr[c, 0:ncp, :] = p[:, c * LANES:(c + 1) * LANES]
                else:
                    ps_scr[c, 0:ncp, :] += p[:, c * LANES:(c + 1) * LANES]
        lanes = [jnp.concatenate([ps_scr[c, pl.ds(r, n_slc, stride=_CMP_PER_SLC), :]
                                  for c in range(tq // LANES)], axis=1) for r in range(_CMP_PER_SLC)]
        row = lax.broadcasted_iota(jnp.int32, (n_slc, tq), 0)
        before = jnp.where(row == 0, 0.0, pltpu.roll(lanes[3], 1, 0))
        imp = before + 2.0 * (lanes[0] + lanes[1] + lanes[2]) + lanes[3]
        score = jnp.where(forced, -jnp.inf, jnp.where(eligible, imp, -SEL_BIG))
        for _ in range(SLC_TOPN - _FORCED):
            best = jnp.max(score, axis=0, keepdims=True)
            first = jnp.min(jnp.where(score == best, blk, SLC_SLOTS), axis=0, keepdims=True)
            score = jnp.where(blk == first, -jnp.inf, score)
        mask_ref[g, 0:n_slc, :] = jnp.where(score == -jnp.inf, 0.0, -MASK_BIG).astype(mask_ref.dtype)
        if n_slc < SLC_SLOTS:
            mask_ref[g, n_slc:, :] = jnp.full((SLC_SLOTS - n_slc, tq), -MASK_BIG, mask_ref.dtype)


def _select(n_q, cmp, cmp_t, tq):
    B, S, _ = n_q.shape
    G = NSA_KV_GROUPS
    ncp = cmp.shape[2]
    return pl.pallas_call(
        _select_kernel,
        grid=(B, S // tq),
        in_specs=[
            pl.BlockSpec((None, tq, N_Q), lambda b, i: (b, i, 0)),
            pl.BlockSpec((None, None, ncp, LANES), lambda b, i: (b, 0, 0, 0)),
            pl.BlockSpec((None, None, LANES, ncp), lambda b, i: (b, 1, 0, 0)),
        ],
        out_specs=[
            pl.BlockSpec((None, N_Q, tq), lambda b, i: (b, 0, i)),
            pl.BlockSpec((None, N_Q, tq), lambda b, i: (b, 0, i)),
            pl.BlockSpec((None, G, SLC_SLOTS, tq), lambda b, i: (b, 0, 0, i)),
        ],
        out_shape=[
            jax.ShapeDtypeStruct((B, N_Q, S), BF16),
            jax.ShapeDtypeStruct((B, N_Q, S), BF16),
            jax.ShapeDtypeStruct((B, G, SLC_SLOTS, S), BF16),
        ],
        scratch_shapes=[pltpu.VMEM((tq // LANES, ncp, LANES), F32)],
        compiler_params=_cparams(("parallel", "parallel")),
        name="nsa_select",
    )(n_q, cmp, cmp_t)


_TK = 256
_WT = 256
_KAUG_W = SLC_SLOTS + N_KV
_AHEAD = 7
_WINDOW_AHEAD = 2
_TILES_PER_TRIP = 8


def _pipelined(work, score, consume):
    pending = [score(item) for item in work[:_AHEAD]]
    for i, item in enumerate(work):
        s = pending.pop(0)
        if i + _AHEAD < len(work):
            pending.append(score(work[i + _AHEAD]))
        consume(item, s)


def _attend_kernel(qt_ref, mask_ref, ocmp_ref, gate_ref, kaug_ref, vt_ref, kw_ref, vwt_ref, y_ref,
                   qa_scr, m_scr, acc_scr, yt_scr):
    tq = qt_ref.shape[1]
    S = kaug_ref.shape[0]
    G = NSA_KV_GROUPS
    q0 = pl.program_id(1) * tq
    t_q = q0 + lax.broadcasted_iota(jnp.int32, (1, tq), 1)
    n_full = q0 // _TK
    kd = pl.multiple_of(n_full * _TK, _TK)
    causal = (kd + lax.broadcasted_iota(jnp.int32, (_TK, 1), 0)) <= t_q
    heads = [(g, hh) for g in range(G) for hh in range(NSA_HPG)]

    for g, hh in heads:
        hd = g * NSA_HPG + hh
        cols = slice(hh * tq, (hh + 1) * tq)
        qa_scr[g, 0:SLC_SLOTS, cols] = mask_ref[g]
        qa_scr[g, SLC_SLOTS:, cols] = _group_rows(qt_ref[hd * NSA_DH:(hd + 1) * NSA_DH, :], g)

    m_scr[...] = jnp.full(m_scr.shape, NEG_BIG, F32)
    acc_scr[...] = jnp.zeros_like(acc_scr)

    def selected_scores(item):
        (g, hh), k0 = item
        return _dot(kaug_ref[pl.ds(k0, _TK), :], qa_scr[g, :, hh * tq:(hh + 1) * tq])

    def selected_tiles(starts, diagonal):
        def consume(item, s):
            (g, hh), k0 = item
            cols = slice(hh * tq, (hh + 1) * tq)
            if diagonal:
                s = jnp.where(causal, s, -MASK_BIG)
            m_old = m_scr[g, :, cols]
            m_new = jnp.maximum(m_old, jnp.max(s, axis=0, keepdims=True).astype(BF16).astype(F32))
            p = jnp.exp(s.astype(BF16) - m_new[0:1].astype(BF16))
            acc_scr[g, :, cols] = (jnp.exp(m_old - m_new)[0:1] * acc_scr[g, :, cols]
                                   + _dot(vt_ref[g, :, pl.ds(k0, _TK)], p))
            m_scr[g, :, cols] = m_new
        _pipelined([(head, k0) for k0 in starts for head in heads], selected_scores, consume)

    def run_tiles(first, count):
        selected_tiles([pl.multiple_of(first + n * _TK, _TK) for n in range(count)], False)

    def stretch(j, carry):
        run_tiles(j * (_TILES_PER_TRIP * _TK), _TILES_PER_TRIP)
        return carry

    lax.fori_loop(0, n_full // _TILES_PER_TRIP, stretch, 0)
    count = _TILES_PER_TRIP // 2
    while count:
        pl.when(n_full & count != 0)(functools.partial(
            run_tiles, (n_full - n_full % (2 * count)) * _TK, count))
        count //= 2
    selected_tiles([kd], True)

    gates_t = _sigmoid(gate_ref[...]).T
    wlen = min(tq + WINDOW, S)
    w0 = pl.multiple_of(jnp.maximum(q0 - WINDOW, 0), tq)

    n_wt = wlen // _WT
    w_tile = [pl.multiple_of(w0 + j * _WT, _WT) for j in range(n_wt)]

    def window_scores(g, hh):
        q_cols = qa_scr[g, SLC_SLOTS:, hh * tq:(hh + 1) * tq]
        return [_dot(kw_ref[pl.ds(w_tile[j], _WT), :], q_cols) for j in range(n_wt)]

    def window(band_mask):
        pending = [window_scores(*head) for head in heads[:_WINDOW_AHEAD]]
        for i, (g, hh) in enumerate(heads):
            s = band_mask(jnp.concatenate(pending.pop(0), axis=0))
            if i + _WINDOW_AHEAD < len(heads):
                pending.append(window_scores(*heads[i + _WINDOW_AHEAD]))
            hd = g * NSA_HPG + hh
            rows_h = slice(hd * NSA_DH, (hd + 1) * NSA_DH)
            e = jnp.exp(s.astype(BF16) - jnp.max(s, axis=0, keepdims=True).astype(BF16))
            ow = sum(_dot(vwt_ref[g, :, pl.ds(w_tile[j], _WT)], e[j * _WT:(j + 1) * _WT])
                     for j in range(n_wt))
            acc = acc_scr[g, :, hh * tq:(hh + 1) * tq]
            gr = GATE_COL0 + 3 * hd
            yt_scr[rows_h, :] = (gates_t[gr:gr + 1] * ocmp_ref[rows_h, :].astype(F32)
                                 + gates_t[gr + 1:gr + 2] * (acc[:NSA_DH] / acc[NSA_DH:NSA_DH + 1])
                                 + gates_t[gr + 2:gr + 3] * (ow[:NSA_DH] / ow[NSA_DH:NSA_DH + 1]))

    def general_band(s):
        kpos = w0 + lax.broadcasted_iota(jnp.int32, (wlen, 1), 0)
        return jnp.where((kpos <= t_q) & (kpos > t_q - WINDOW), s, NEG_BIG)

    if wlen == tq + WINDOW and WINDOW % tq == 0:
        r = lax.broadcasted_iota(jnp.int32, (tq, tq), 0)
        c = lax.broadcasted_iota(jnp.int32, (tq, tq), 1)

        def interior_band(s):
            return jnp.concatenate([jnp.where(r > c, s[:tq], NEG_BIG), s[tq:WINDOW],
                                    jnp.where(r <= c, s[WINDOW:], NEG_BIG)], axis=0)

        pl.when(q0 >= WINDOW)(functools.partial(window, interior_band))
        pl.when(q0 < WINDOW)(functools.partial(window, general_band))
    else:
        window(general_band)
    y_ref[...] = yt_scr[...].T.astype(y_ref.dtype)


def _attend(q_t, mask_t, ocmp_t, narrow, kaug, v_t, kw, vw_t, tq):
    B, _, S = q_t.shape
    G = NSA_KV_GROUPS
    rows = NSA_HPG * tq
    return pl.pallas_call(
        _attend_kernel,
        grid=(B, S // tq),
        in_specs=[
            pl.BlockSpec((None, N_Q, tq), lambda b, i: (b, 0, i)),
            pl.BlockSpec((None, G, SLC_SLOTS, tq), lambda b, i: (b, 0, 0, i)),
            pl.BlockSpec((None, N_Q, tq), lambda b, i: (b, 0, i)),
            pl.BlockSpec((None, tq, _NARROW_W), lambda b, i: (b, i, 0)),
            pl.BlockSpec((None, S, _KAUG_W), lambda b, i: (b, 0, 0)),
            pl.BlockSpec((None, G, _VT_ROWS, S), lambda b, i: (b, 0, 0, 0)),
            pl.BlockSpec((None, S, N_KV), lambda b, i: (b, 0, 0)),
            pl.BlockSpec((None, G, _VT_ROWS, S), lambda b, i: (b, 0, 0, 0)),
        ],
        out_specs=pl.BlockSpec((None, tq, N_Q), lambda b, i: (b, i, 0)),
        out_shape=jax.ShapeDtypeStruct((B, S, N_Q), BF16),
        scratch_shapes=[
            pltpu.VMEM((G, _KAUG_W, rows), BF16),
            pltpu.VMEM((G, 8, rows), F32),
            pltpu.VMEM((G, _VT_ROWS, rows), F32),
            pltpu.VMEM((N_Q, tq), F32),
        ],
        compiler_params=_cparams(("parallel", "arbitrary")),
        name="nsa_attend",
    )(q_t, mask_t, ocmp_t, narrow, kaug, v_t, kw, vw_t)


def _merge_kernel(x_ref, ya_ref, yb_ref, ga_ref, gb_ref, wa_ref, wb_ref, wo_ref, g2_ref, x1_ref, h2_ref):
    a = _sigmoid(ga_ref[...].astype(F32)) * _dot(ya_ref[...], wa_ref[...])
    b = _sigmoid(gb_ref[...].astype(F32)) * _dot(yb_ref[...], wb_ref[...])
    x1 = x_ref[...] + _dot((a + b).astype(BF16), wo_ref[...])
    x1_ref[...] = x1
    h2_ref[...] = _rms(x1, g2_ref[...]).astype(h2_ref.dtype)


def _merge(x2d, ya, yb, ga, gb, wa, wb, wo, g2, tm):
    T = x2d.shape[0]
    row = lambda w: pl.BlockSpec((tm, w), lambda i: (i, 0))
    const = lambda r, c: pl.BlockSpec((r, c), lambda i: (0, 0))
    return pl.pallas_call(
        _merge_kernel,
        grid=(T // tm,),
        in_specs=[row(D_MODEL), row(M_V), row(N_Q), row(D_MODEL), row(D_MODEL),
                  const(M_V, D_MODEL), const(N_Q, D_MODEL), const(D_MODEL, D_MODEL), const(1, D_MODEL)],
        out_specs=[row(D_MODEL), row(D_MODEL)],
        out_shape=[jax.ShapeDtypeStruct((T, D_MODEL), F32), jax.ShapeDtypeStruct((T, D_MODEL), BF16)],
        compiler_params=_cparams(("parallel",)),
        name="merge",
    )(x2d, ya, yb, ga, gb, wa.astype(BF16), wb.astype(BF16), wo.astype(BF16), g2[None, :])


_FFN_CHUNK = 256


def _ffn_kernel(x1_ref, h2_ref, wg_ref, wu_ref, wd_ref, gf_ref, out_ref):
    h = h2_ref[...]
    acc = x1_ref[...]
    for c in range(FFN_HIDDEN // _FFN_CHUNK):
        sl = slice(c * _FFN_CHUNK, (c + 1) * _FFN_CHUNK)
        gate = _dot(h, wg_ref[:, sl])
        up = _dot(h, wu_ref[:, sl])
        acc = acc + _dot((gate * _sigmoid(gate) * up).astype(BF16), wd_ref[sl, :])
    out_ref[...] = _rms(acc, gf_ref[...])


def _ffn(x1, h2, wg, wu, wd, gf, tm):
    T = x1.shape[0]
    row = lambda: pl.BlockSpec((tm, D_MODEL), lambda i: (i, 0))
    const = lambda r, c: pl.BlockSpec((r, c), lambda i: (0, 0))
    return pl.pallas_call(
        _ffn_kernel,
        grid=(T // tm,),
        in_specs=[row(), row(), const(D_MODEL, FFN_HIDDEN), const(D_MODEL, FFN_HIDDEN),
                  const(FFN_HIDDEN, D_MODEL), const(1, D_MODEL)],
        out_specs=row(),
        out_shape=jax.ShapeDtypeStruct((T, D_MODEL), F32),
        compiler_params=_cparams(("parallel",)),
        name="ffn",
    )(x1, h2, wg.astype(BF16), wu.astype(BF16), wd.astype(BF16), gf[None, :])


def _layer(x, norm1_g, w_in, b_in, f_bias, conv_w, conv_b, mlstm_norm_g, cmp_k_pos, cmp_k_w1, cmp_k_w2,
           cmp_v_pos, cmp_v_w1, cmp_v_w2, w_branch_a, w_branch_b, w_out, norm2_g):
    B, S, D = x.shape
    T = B * S
    tm = min(256, S)

    (m_q, m_k, m_v, m_o, n_q, gate_a, gate_b, narrow, narrow_t, cmp_kv, kaug, vs_t, k_win, vw_t) = _inproj(
        x, norm1_g, w_in, b_in, conv_w, conv_b, tm)
    narrow = narrow.reshape(B, S, _NARROW_W)

    y_a = _mlstm(m_q.reshape(B, S, M_QK), m_k.reshape(B, S, M_QK), m_v.reshape(B, S, M_V),
                 m_o.reshape(B, S, M_V), narrow, narrow_t, f_bias, mlstm_norm_g, min(512, S))

    cmp, cmp_t = _compress(cmp_kv.reshape(B, S, 2 * N_KV), jnp.stack([cmp_k_pos, cmp_v_pos]),
                           jnp.stack([cmp_k_w1, cmp_v_w1]), jnp.stack([cmp_k_w2, cmp_v_w2]))
    tq = min(256, S)
    q_t, ocmp_t, mask_t = _select(n_q.reshape(B, S, N_Q), cmp, cmp_t, tq)
    y_b = _attend(q_t, mask_t, ocmp_t, narrow, kaug.reshape(B, S, _KAUG_W), vs_t,
                  k_win.reshape(B, S, N_KV), vw_t, tq)

    return _merge(x.reshape(T, D), y_a.reshape(T, M_V), y_b.reshape(T, N_Q), gate_a, gate_b,
                  w_branch_a, w_branch_b, w_out, norm2_g, min(512, T))


def kernel(x, norm1_g, w_in, b_in, f_bias, conv_w, conv_b, mlstm_norm_g, cmp_k_pos, cmp_k_w1, cmp_k_w2,
           cmp_v_pos, cmp_v_w1, cmp_v_w2, w_branch_a, w_branch_b, w_out, norm2_g, w_ffn_gate, w_ffn_up,
           w_ffn_down, norm_f_g):
    B, S, D = x.shape
    depth = w_in.shape[0]
    assert depth == 1, "the fused final norm assumes a single layer"
    x1, h2 = _layer(x, norm1_g[0], w_in[0], b_in[0], f_bias[0], conv_w[0], conv_b[0], mlstm_norm_g[0],
                    cmp_k_pos[0], cmp_k_w1[0], cmp_k_w2[0], cmp_v_pos[0], cmp_v_w1[0], cmp_v_w2[0],
                    w_branch_a[0], w_branch_b[0], w_out[0], norm2_g[0])
    out = _ffn(x1, h2, w_ffn_gate[0], w_ffn_up[0], w_ffn_down[0], norm_f_g, min(512, B * S))
    return out.reshape(B, S, D)
```

```python
import functools

import numpy as np
import jax
import jax.numpy as jnp
from jax import lax
from jax.experimental import pallas as pl
from jax.experimental.pallas import tpu as pltpu

F32 = jnp.float32
BF16 = jnp.bfloat16

D_MODEL = 1024
MLSTM_HEADS = 4
MLSTM_DV = D_MODEL // MLSTM_HEADS
MLSTM_DQK = MLSTM_DV // 2
CONV_K = 4
NSA_DH = 64
NSA_HEADS = (D_MODEL // 2) // NSA_DH
NSA_KV_GROUPS = 2
NSA_HPG = NSA_HEADS // NSA_KV_GROUPS
CMP_BLOCK = 32
CMP_STRIDE = 16
CMP_HIDDEN = 256
SLC_BLOCK = 64
SLC_TOPN = 16
WINDOW = 512
FFN_HIDDEN = 2816
RMS_EPS = 1e-6
SEL_BIG = 1e9

M_QK = MLSTM_HEADS * MLSTM_DQK
M_V = MLSTM_HEADS * MLSTM_DV
N_Q = NSA_HEADS * NSA_DH
N_KV = NSA_KV_GROUPS * NSA_DH
IN_SPLITS = (2 * M_QK, M_V, M_V, MLSTM_HEADS, MLSTM_HEADS, N_Q, 6 * N_KV, 3 * NSA_HEADS, D_MODEL, D_MODEL)

LANES = 128
SLC_SLOTS = 128
MASK_BIG = float(2.0 ** 100)
NEG_BIG = -1e30
VMEM_LIMIT = 56 * 1024 * 1024
assert N_KV == LANES, "both KV groups are packed into one 128-lane key row"

_SEG = dict(zip(("m_qk", "m_v", "m_o", "m_i", "m_f", "n_q", "n_kv", "n_g", "gate_a", "gate_b"),
                zip(np.cumsum((0,) + IN_SPLITS[:-1]).tolist(), IN_SPLITS)))
_WIDE = ("m_qk", "m_v", "m_o", "n_q", "gate_a", "gate_b", "n_kv")
_NARROW = ("m_i", "m_f", "n_g")
_NARROW_W = LANES
GATE_COL0 = 2 * MLSTM_HEADS
_VT_ROWS = 80


def _cparams(sem):
    return pltpu.CompilerParams(dimension_semantics=sem, vmem_limit_bytes=VMEM_LIMIT)


def _sigmoid(x):
    return 1.0 / (1.0 + jnp.exp(-x))


def _log_sigmoid(x):
    return jnp.minimum(x, 0.0) - jnp.log(1.0 + jnp.exp(-jnp.abs(x)))


def _rms(x, g):
    ms = jnp.mean(x * x, axis=-1, keepdims=True)
    return x * lax.rsqrt(ms + RMS_EPS) * g


def _dot_nt(a, b):
    return lax.dot_general(a, b, (((1,), (1,)), ((), ())), preferred_element_type=F32)


def _dot(a, b):
    return jnp.dot(a, b, preferred_element_type=F32)


def _values_t(v):
    v_t = v.T
    row = lax.broadcasted_iota(jnp.int32, (_VT_ROWS - NSA_DH, v.shape[0]), 0)
    tail = jnp.where(row == 0, 1.0, 0.0)
    return [jnp.concatenate([v_t[g * NSA_DH:(g + 1) * NSA_DH], tail], axis=0).astype(BF16)
            for g in range(NSA_KV_GROUPS)]


_HALO = 8
_CONV_CHUNK = 2 * M_QK // 4
assert M_QK % _CONV_CHUNK == 0
_CONV_ROWS = 256


def _inproj_kernel(x_ref, g_ref, w_ref, b_ref, cw_ref, cb_ref, q_ref, k_ref, v_ref, o_ref, nq_ref, ga_ref,
                   gb_ref, narrow_ref, narrow_t_ref, cmp_ref, kaug_ref, vst_ref, kw_ref, vwt_ref, halo_scr,
                   *, offs, tiles_per_seq):
    tm = x_ref.shape[0]
    first_of_seq = pl.program_id(0) % tiles_per_seq == 0
    h = _rms(x_ref[...], g_ref[...]).astype(BF16)

    def seg(off, width):
        return _dot(h, w_ref[:, off:off + width]) + b_ref[:, off:off + width]

    def conv_chunk(c):
        cols = slice(c * _CONV_CHUNK, (c + 1) * _CONV_CHUNK)
        qk = seg(offs["m_qk"] + c * _CONV_CHUNK, _CONV_CHUNK)
        halo = jnp.where(first_of_seq, 0.0, halo_scr[:, cols])
        halo_scr[:, cols] = qk[tm - _HALO:]
        for r0 in range(0, tm, _CONV_ROWS):
            rows = slice(r0, r0 + _CONV_ROWS)
            piece = qk[rows]
            padded = jnp.concatenate([halo if r0 == 0 else qk[r0 - _HALO:r0], piece], axis=0)
            acc = piece * cw_ref[CONV_K - 1:CONV_K, cols] + cb_ref[:, cols]
            for j in range(1, CONV_K):
                acc = acc + pltpu.roll(padded, j, 0)[_HALO:] * cw_ref[CONV_K - 1 - j:CONV_K - j, cols]
            act = acc * _sigmoid(acc)
            if c * _CONV_CHUNK < M_QK:
                q_ref[rows, cols] = act.astype(BF16)
            else:
                k_ref[rows, c * _CONV_CHUNK - M_QK:(c + 1) * _CONV_CHUNK - M_QK] = (
                    act * (MLSTM_DQK ** -0.5)).astype(BF16)

    others = [(v_ref, offs["m_v"], M_V), (o_ref, offs["m_o"], M_V), (ga_ref, offs["gate_a"], D_MODEL),
              (gb_ref, offs["gate_b"], D_MODEL)]
    per = 2 * M_QK // _CONV_CHUNK // len(others)
    for n, (ref, off, width) in enumerate(others):
        for c in range(n * per, (n + 1) * per):
            conv_chunk(c)
        ref[...] = seg(off, width).astype(BF16)
    nq_ref[...] = seg(offs["n_q"], N_Q).astype(BF16)

    assert offs["narrow"] == offs["n_kv"] + 6 * N_KV
    kv = seg(offs["n_kv"], 6 * N_KV + _NARROW_W)
    part = lambda j, n=1: kv[:, j * N_KV:(j + n) * N_KV]
    narrow = kv[:, 6 * N_KV:]
    narrow_ref[...] = narrow
    narrow_t_ref[...] = narrow.T
    cmp_ref[...] = part(0, 2)
    pos = (pl.program_id(0) % tiles_per_seq) * tm + lax.broadcasted_iota(jnp.int32, (tm, SLC_SLOTS), 0)
    onehot = (pos // SLC_BLOCK == lax.broadcasted_iota(jnp.int32, (tm, SLC_SLOTS), 1)).astype(BF16)
    kaug_ref[...] = jnp.concatenate([onehot, part(2).astype(BF16)], axis=1)
    for g, vt in enumerate(_values_t(part(3))):
        vst_ref[g] = vt
    kw_ref[...] = part(4).astype(BF16)
    for g, vt in enumerate(_values_t(part(5))):
        vwt_ref[g] = vt


def _inproj(x, norm_g, w_in, b_in, conv_w, conv_b, tm):
    B, S, _ = x.shape
    T = B * S
    G = NSA_KV_GROUPS
    cols = [w_in[:, _SEG[n][0]:_SEG[n][0] + _SEG[n][1]] for n in _WIDE]
    bias = [b_in[_SEG[n][0]:_SEG[n][0] + _SEG[n][1]] for n in _WIDE]
    narrow_w = jnp.concatenate([w_in[:, _SEG[n][0]:_SEG[n][0] + _SEG[n][1]] for n in _NARROW], axis=1)
    narrow_b = jnp.concatenate([b_in[_SEG[n][0]:_SEG[n][0] + _SEG[n][1]] for n in _NARROW])
    pad = _NARROW_W - narrow_w.shape[1]
    cols.append(jnp.pad(narrow_w, ((0, 0), (0, pad))))
    bias.append(jnp.pad(narrow_b, (0, pad)))
    w = jnp.concatenate(cols, axis=1).astype(BF16)
    b = jnp.concatenate(bias)[None, :].astype(F32)
    widths = [_SEG[n][1] for n in _WIDE] + [_NARROW_W]
    offs = dict(zip(_WIDE + ("narrow",), np.cumsum([0] + widths[:-1]).tolist()))
    n_tot = w.shape[1]
    tps = S // tm

    def rows(width, dtype):
        return pl.BlockSpec((tm, width), lambda i: (i, 0)), jax.ShapeDtypeStruct((T, width), dtype)

    def rows_t(lead, dtype):
        nd = len(lead)
        return (pl.BlockSpec((None,) + lead + (tm,), lambda i: (i // tps,) + (0,) * nd + (i % tps,)),
                jax.ShapeDtypeStruct((B,) + lead + (S,), dtype))

    outs = [rows(M_QK, BF16), rows(M_QK, BF16), rows(M_V, BF16), rows(M_V, BF16), rows(N_Q, BF16),
            rows(D_MODEL, BF16), rows(D_MODEL, BF16), rows(_NARROW_W, F32), rows_t((_NARROW_W,), F32),
            rows(2 * N_KV, F32), rows(SLC_SLOTS + N_KV, BF16), rows_t((G, _VT_ROWS), BF16), rows(N_KV, BF16),
            rows_t((G, _VT_ROWS), BF16)]
    return pl.pallas_call(
        functools.partial(_inproj_kernel, offs=offs, tiles_per_seq=tps),
        grid=(T // tm,),
        in_specs=[
            pl.BlockSpec((tm, D_MODEL), lambda i: (i, 0)),
            pl.BlockSpec((1, D_MODEL), lambda i: (0, 0)),
            pl.BlockSpec((D_MODEL, n_tot), lambda i: (0, 0)),
            pl.BlockSpec((1, n_tot), lambda i: (0, 0)),
            pl.BlockSpec((CONV_K, 2 * M_QK), lambda i: (0, 0)),
            pl.BlockSpec((1, 2 * M_QK), lambda i: (0, 0)),
        ],
        out_specs=[o[0] for o in outs],
        out_shape=[o[1] for o in outs],
        scratch_shapes=[pltpu.VMEM((_HALO, 2 * M_QK), F32)],
        compiler_params=_cparams(("arbitrary",)),
        name="inproj",
    )(x.reshape(T, D_MODEL), norm_g[None, :], w, b, conv_w, conv_b[None, :])


def _mlstm_kernel(q_ref, k_ref, v_ref, o_ref, gr_ref, gc_ref, fb_ref, fbl_ref, ng_ref, y_ref, c_scr, m_scr):
    H, DK, DV = MLSTM_HEADS, MLSTM_DQK, MLSTM_DV
    L = q_ref.shape[1]
    chains = [(bi, h) for bi in range(q_ref.shape[0]) for h in range(H)]
    heads = range(len(chains))

    @pl.when(pl.program_id(1) == 0)
    def _():
        c_scr[...] = jnp.zeros_like(c_scr)
        m_scr[...] = jnp.zeros_like(m_scr)

    r_t = lax.broadcasted_iota(jnp.int32, (L, L), 0)
    r_s = lax.broadcasted_iota(jnp.int32, (L, L), 1)
    causal = r_s <= r_t
    upper = r_t <= r_s
    gates = [gc_ref[bi] for bi in range(q_ref.shape[0])]
    lf_cols = [_log_sigmoid(g + fbl_ref[...]) for g in gates]

    qs = [q_ref[bi, :, h * DK:(h + 1) * DK] for bi, h in chains]
    ks = [k_ref[bi, :, h * DK:(h + 1) * DK] for bi, h in chains]
    qk = [_dot_nt(qs[n], ks[n]) for n in heads]
    v_ext = [jnp.concatenate([v_ref[bi, :, h * DV:(h + 1) * DV], jnp.ones((L, LANES), BF16)], axis=1)
             for bi, h in chains]
    state = [c_scr[n] for n in heads]
    inter = [_dot(qs[n], state[n].astype(BF16)) for n in heads]

    stab = []
    for n, (bi, h) in enumerate(chains):
        fb = fb_ref[h:h + 1, 0:1]
        i_row = gr_ref[bi, h:h + 1, :]
        lf_row = _log_sigmoid(gr_ref[bi, H + h:H + h + 1, :] + fb)
        i_col = gates[bi][:, h:h + 1]
        lf_col = lf_cols[bi][:, H + h:H + h + 1]
        m_prev = m_scr[n]
        b_col = jnp.sum(jnp.where(causal, lf_row, 0.0), axis=1, keepdims=True)
        b_row = jnp.sum(jnp.where(upper, lf_col, 0.0), axis=0, keepdims=True)
        b_last = jnp.sum(lf_row, axis=1, keepdims=True)
        d_intra = jnp.where(causal, b_col - b_row + i_row, NEG_BIG)
        d_inter = b_col + m_prev
        m_t = jnp.maximum(d_inter, jnp.max(d_intra, axis=1, keepdims=True))
        d_state_row = b_last - b_row + i_row
        m_new = jnp.maximum(b_last + m_prev, jnp.max(d_state_row, axis=1, keepdims=True))
        stab.append(dict(w_intra=jnp.exp(d_intra - m_t), w_inter=jnp.exp(d_inter - m_t), floor=jnp.exp(-m_t),
                         w_state=jnp.exp(b_last - b_col + i_col - m_new),
                         decay=jnp.exp(b_last + m_prev - m_new), m_new=m_new))

    s = [(qk[n] * stab[n]["w_intra"]).astype(BF16) for n in heads]
    num = [_dot(s[n], v_ext[n]) + stab[n]["w_inter"] * inter[n] for n in heads]
    for n, (bi, h) in enumerate(chains):
        cols = slice(h * DV, (h + 1) * DV)
        hid = num[n][:, :DV] / jnp.maximum(jnp.abs(num[n][:, DV:DV + 1]), stab[n]["floor"])
        hid = _rms(hid, ng_ref[:, cols])
        y_ref[bi, :, cols] = (_sigmoid(o_ref[bi, :, cols].astype(F32)) * hid).astype(y_ref.dtype)
    k_t = [ks[n].astype(F32).T.astype(BF16) for n in heads]
    vw = [v_ext[n] * stab[n]["w_state"].astype(BF16) for n in heads]
    for n in heads:
        c_scr[n] = stab[n]["decay"] * state[n] + _dot(k_t[n], vw[n])
        m_scr[n] = stab[n]["m_new"]


def _mlstm(q, k, v, o, narrow, narrow_t, f_bias, norm_g, L):
    B, S, _ = q.shape
    H, DK, DV = MLSTM_HEADS, MLSTM_DQK, MLSTM_DV
    fb = jnp.broadcast_to(f_bias.astype(F32)[:, None], (H, LANES))
    fb_lanes = jnp.zeros((1, _NARROW_W), F32).at[0, H:2 * H].set(f_bias.astype(F32))
    nb = 2 if B % 2 == 0 else 1
    return pl.pallas_call(
        _mlstm_kernel,
        grid=(B // nb, S // L),
        in_specs=[
            pl.BlockSpec((nb, L, M_QK), lambda b, c: (b, c, 0)),
            pl.BlockSpec((nb, L, M_QK), lambda b, c: (b, c, 0)),
            pl.BlockSpec((nb, L, M_V), lambda b, c: (b, c, 0)),
            pl.BlockSpec((nb, L, M_V), lambda b, c: (b, c, 0)),
            pl.BlockSpec((nb, 2 * H, L), lambda b, c: (b, 0, c)),
            pl.BlockSpec((nb, L, _NARROW_W), lambda b, c: (b, c, 0)),
            pl.BlockSpec((H, LANES), lambda b, c: (0, 0)),
            pl.BlockSpec((1, _NARROW_W), lambda b, c: (0, 0)),
            pl.BlockSpec((1, M_V), lambda b, c: (0, 0)),
        ],
        out_specs=pl.BlockSpec((nb, L, M_V), lambda b, c: (b, c, 0)),
        out_shape=jax.ShapeDtypeStruct((B, S, M_V), BF16),
        scratch_shapes=[pltpu.VMEM((nb * H, DK, DV + LANES), F32), pltpu.VMEM((nb * H, 1, 1), F32)],
        compiler_params=_cparams(("parallel", "arbitrary")),
        name="mlstm",
    )(q, k, v, o, narrow_t, narrow, fb, fb_lanes, norm_g[None, :])


def _compress_kernel(x_ref, pos_ref, w1_ref, w2_ref, out_ref, out_t_ref, x_scr):
    S = x_ref.shape[0]
    n = S // CMP_STRIDE
    x_scr[0:S] = x_ref[...]
    x_scr[S:] = jnp.zeros((CMP_STRIDE, LANES), F32)
    acc = jnp.zeros((n, w1_ref.shape[2]), F32)
    for l in range(CMP_BLOCK):
        tok = x_scr[pl.ds(l, n, stride=CMP_STRIDE), :] + pos_ref[l:l + 1, :]
        acc = acc + _dot(tok.astype(BF16), w1_ref[l])
    hid = acc * _sigmoid(acc)
    out = _dot(hid.astype(BF16), w2_ref[...])
    row = lax.broadcasted_iota(jnp.int32, out.shape, 0)
    out = jnp.where(row < n - 1, out, 0.0)
    out_ref[...] = out.astype(out_ref.dtype)
    out_t_ref[...] = out.T.astype(out_t_ref.dtype)


def _block_diag2(w):
    z = jnp.zeros_like(w)
    return jnp.concatenate([jnp.concatenate([w, z], axis=-1), jnp.concatenate([z, w], axis=-1)], axis=-2)


def _compress(cmp_kv, pos, w1, w2):
    B, S, _ = cmp_kv.shape
    n = S // CMP_STRIDE
    pos2 = jnp.concatenate([pos, pos], axis=-1)
    w1b = _block_diag2(w1.reshape(2, CMP_BLOCK, NSA_DH, CMP_HIDDEN)).astype(BF16)
    w2b = _block_diag2(w2).astype(BF16)
    return pl.pallas_call(
        _compress_kernel,
        grid=(B, 2),
        in_specs=[
            pl.BlockSpec((None, S, LANES), lambda b, a: (b, 0, a)),
            pl.BlockSpec((None, CMP_BLOCK, LANES), lambda b, a: (a, 0, 0)),
            pl.BlockSpec((None, CMP_BLOCK, LANES, 2 * CMP_HIDDEN), lambda b, a: (a, 0, 0, 0)),
            pl.BlockSpec((None, 2 * CMP_HIDDEN, LANES), lambda b, a: (a, 0, 0)),
        ],
        out_specs=[pl.BlockSpec((None, None, n, LANES), lambda b, a: (b, a, 0, 0)),
                   pl.BlockSpec((None, None, LANES, n), lambda b, a: (b, a, 0, 0))],
        out_shape=[jax.ShapeDtypeStruct((B, 2, n, LANES), BF16),
                   jax.ShapeDtypeStruct((B, 2, LANES, n), BF16)],
        scratch_shapes=[pltpu.VMEM((S + CMP_STRIDE, LANES), F32)],
        compiler_params=_cparams(("parallel", "parallel")),
        name="nsa_compress",
    )(cmp_kv, pos2, w1b, w2b)


_CMP_PER_SLC = SLC_BLOCK // CMP_STRIDE
_FORCED = 3
_LOG2E = 1.4426950408889634


def _group_rows(q_h, g):
    z = jnp.zeros_like(q_h)
    return jnp.concatenate([q_h, z] if g == 0 else [z, q_h], axis=0)


def _select_kernel(q_ref, kc_ref, vct_ref, qt_ref, ocmp_ref, mask_ref, ps_scr):
    tq = q_ref.shape[0]
    t0 = pl.program_id(1) * tq
    q_f = q_ref[...].astype(F32).T * (NSA_DH ** -0.5)
    qt_ref[...] = q_f.astype(BF16)
    q_t = (q_f * _LOG2E).astype(BF16)
    need = (t0 + tq) // CMP_STRIDE
    for nk in range(LANES, kc_ref.shape[0] + 1, LANES):
        pl.when((need - 1) // LANES == nk // LANES - 1)(
            functools.partial(_select_body, q_t, kc_ref, vct_ref, ocmp_ref, mask_ref, ps_scr, t0, nk))


def _select_body(q_t, kc_ref, vct_ref, ocmp_ref, mask_ref, ps_scr, t0, ncp):
    tq = q_t.shape[1]
    n_slc = ncp // _CMP_PER_SLC
    t_row = t0 + lax.broadcasted_iota(jnp.int32, (1, tq), 1)
    n_vis = max(ncp - LANES - tq // CMP_STRIDE - 2 * CMP_BLOCK // CMP_STRIDE, 0) // 8 * 8
    c_col = n_vis + lax.broadcasted_iota(jnp.int32, (ncp - n_vis, 1), 0)
    visible = (c_col * CMP_STRIDE + (CMP_BLOCK - 1)) <= t_row
    any_visible = t_row >= CMP_BLOCK - 1

    blk = lax.broadcasted_iota(jnp.int32, (n_slc, tq), 0)
    cur = (t0 + lax.broadcasted_iota(jnp.int32, (n_slc, tq), 1)) // SLC_BLOCK
    forced = (blk == 0) | (blk == cur) | (blk == cur - 1)
    eligible = blk <= cur

    def scores(hd):
        return _dot(kc_ref[0:ncp, :], _group_rows(q_t[hd * NSA_DH:(hd + 1) * NSA_DH], hd // NSA_HPG))

    s_next = scores(0)
    for g in range(NSA_KV_GROUPS):
        vct = vct_ref[g * NSA_DH:(g + 1) * NSA_DH, 0:ncp]
        for hh in range(NSA_HPG):
            hd = g * NSA_HPG + hh
            s = jnp.where(visible, s_next[n_vis:], NEG_BIG)
            if n_vis:
                s = jnp.concatenate([s_next[:n_vis], s], axis=0)
            if hd + 1 < NSA_HEADS:
                s_next = scores(hd + 1)
            e = jnp.exp2(s - jnp.max(s, axis=0, keepdims=True))
            inv = jnp.where(any_visible, 1.0 / jnp.sum(e, axis=0, keepdims=True), 0.0)
            p = e * inv
            ocmp_ref[hd * NSA_DH:(hd + 1) * NSA_DH, :] = _dot(vct, p.astype(BF16)).astype(ocmp_ref.dtype)
            for c in range(tq // LANES):
                if hh == 0:
                    ps_scr[c, 0:ncp, :] = p[:, c * LANES:(c + 1) * LANES]
                else:
                    ps_scr[c, 0:ncp, :] += p[:, c * LANES:(c + 1) * LANES]
        lanes = [jnp.concatenate([ps_scr[c, pl.ds(r, n_slc, stride=_CMP_PER_SLC), :]
                                  for c in range(tq // LANES)], axis=1) for r in range(_CMP_PER_SLC)]
        row = lax.broadcasted_iota(jnp.int32, (n_slc, tq), 0)
        before = jnp.where(row == 0, 0.0, pltpu.roll(lanes[3], 1, 0))
        imp = before + 2.0 * (lanes[0] + lanes[1] + lanes[2]) + lanes[3]
        score = jnp.where(forced, -jnp.inf, jnp.where(eligible, imp, -SEL_BIG))
        for _ in range(SLC_TOPN - _FORCED):
            best = jnp.max(score, axis=0, keepdims=True)
            first = jnp.min(jnp.where(score == best, blk, SLC_SLOTS), axis=0, keepdims=True)
            score = jnp.where(blk == first, -jnp.inf, score)
        mask_ref[g, 0:n_slc, :] = jnp.where(score == -jnp.inf, 0.0, -MASK_BIG).astype(mask_ref.dtype)
        if n_slc < SLC_SLOTS:
            mask_ref[g, n_slc:, :] = jnp.full((SLC_SLOTS - n_slc, tq), -MASK_BIG, mask_ref.dtype)


def _select(n_q, cmp, cmp_t, tq):
    B, S, _ = n_q.shape
    G = NSA_KV_GROUPS
    ncp = cmp.shape[2]
    return pl.pallas_call(
        _select_kernel,
        grid=(B, S // tq),
        in_specs=[
            pl.BlockSpec((None, tq, N_Q), lambda b, i: (b, i, 0)),
            pl.BlockSpec((None, None, ncp, LANES), lambda b, i: (b, 0, 0, 0)),
            pl.BlockSpec((None, None, LANES, ncp), lambda b, i: (b, 1, 0, 0)),
        ],
        out_specs=[
            pl.BlockSpec((None, N_Q, tq), lambda b, i: (b, 0, i)),
            pl.BlockSpec((None, N_Q, tq), lambda b, i: (b, 0, i)),
            pl.BlockSpec((None, G, SLC_SLOTS, tq), lambda b, i: (b, 0, 0, i)),
        ],
        out_shape=[
            jax.ShapeDtypeStruct((B, N_Q, S), BF16),
            jax.ShapeDtypeStruct((B, N_Q, S), BF16),
            jax.ShapeDtypeStruct((B, G, SLC_SLOTS, S), BF16),
        ],
        scratch_shapes=[pltpu.VMEM((tq // LANES, ncp, LANES), F32)],
        compiler_params=_cparams(("parallel", "parallel")),
        name="nsa_select",
    )(n_q, cmp, cmp_t)


_TK = 256
_WT = 256
_KAUG_W = SLC_SLOTS + N_KV
_AHEAD = 7
_WINDOW_AHEAD = 2
_TILES_PER_TRIP = 8


def _pipelined(work, score, consume):
    pending = [score(item) for item in work[:_AHEAD]]
    for i, item in enumerate(work):
        s = pending.pop(0)
        if i + _AHEAD < len(work):
            pending.append(score(work[i + _AHEAD]))
        consume(item, s)


def _attend_kernel(qt_ref, mask_ref, ocmp_ref, gate_ref, kaug_ref, vt_ref, kw_ref, vwt_ref, y_ref,
                   qa_scr, m_scr, acc_scr, yt_scr):
    tq = qt_ref.shape[1]
    S = kaug_ref.shape[0]
    G = NSA_KV_GROUPS
    q0 = pl.program_id(1) * tq
    t_q = q0 + lax.broadcasted_iota(jnp.int32, (1, tq), 1)
    n_full = q0 // _TK
    kd = pl.multiple_of(n_full * _TK, _TK)
    causal = (kd + lax.broadcasted_iota(jnp.int32, (_TK, 1), 0)) <= t_q
    heads = [(g, hh) for g in range(G) for hh in range(NSA_HPG)]

    for g, hh in heads:
        hd = g * NSA_HPG + hh
        cols = slice(hh * tq, (hh + 1) * tq)
        qa_scr[g, 0:SLC_SLOTS, cols] = mask_ref[g]
        qa_scr[g, SLC_SLOTS:, cols] = _group_rows(qt_ref[hd * NSA_DH:(hd + 1) * NSA_DH, :], g)

    m_scr[...] = jnp.full(m_scr.shape, NEG_BIG, F32)
    acc_scr[...] = jnp.zeros_like(acc_scr)

    def selected_scores(item):
        (g, hh), k0 = item
        return _dot(kaug_ref[pl.ds(k0, _TK), :], qa_scr[g, :, hh * tq:(hh + 1) * tq])

    def selected_tiles(starts, diagonal):
        def consume(item, s):
            (g, hh), k0 = item
            cols = slice(hh * tq, (hh + 1) * tq)
            if diagonal:
                s = jnp.where(causal, s, -MASK_BIG)
            m_old = m_scr[g, :, cols]
            m_new = jnp.maximum(m_old, jnp.max(s, axis=0, keepdims=True).astype(BF16).astype(F32))
            p = jnp.exp(s.astype(BF16) - m_new[0:1].astype(BF16))
            acc_scr[g, :, cols] = (jnp.exp(m_old - m_new)[0:1] * acc_scr[g, :, cols]
                                   + _dot(vt_ref[g, :, pl.ds(k0, _TK)], p))
            m_scr[g, :, cols] = m_new
        _pipelined([(head, k0) for k0 in starts for head in heads], selected_scores, consume)

    def run_tiles(first, count):
        selected_tiles([pl.multiple_of(first + n * _TK, _TK) for n in range(count)], False)

    def stretch(j, carry):
        run_tiles(j * (_TILES_PER_TRIP * _TK), _TILES_PER_TRIP)
        return carry

    lax.fori_loop(0, n_full // _TILES_PER_TRIP, stretch, 0)
    count = _TILES_PER_TRIP // 2
    while count:
        pl.when(n_full & count != 0)(functools.partial(
            run_tiles, (n_full - n_full % (2 * count)) * _TK, count))
        count //= 2
    selected_tiles([kd], True)

    gates_t = _sigmoid(gate_ref[...]).T
    wlen = min(tq + WINDOW, S)
    w0 = pl.multiple_of(jnp.maximum(q0 - WINDOW, 0), tq)

    n_wt = wlen // _WT
    w_tile = [pl.multiple_of(w0 + j * _WT, _WT) for j in range(n_wt)]

    def window_scores(g, hh):
        q_cols = qa_scr[g, SLC_SLOTS:, hh * tq:(hh + 1) * tq]
        return [_dot(kw_ref[pl.ds(w_tile[j], _WT), :], q_cols) for j in range(n_wt)]

    def window(band_mask):
        pending = [window_scores(*head) for head in heads[:_WINDOW_AHEAD]]
        for i, (g, hh) in enumerate(heads):
            s = band_mask(jnp.concatenate(pending.pop(0), axis=0))
            if i + _WINDOW_AHEAD < len(heads):
                pending.append(window_scores(*heads[i + _WINDOW_AHEAD]))
            hd = g * NSA_HPG + hh
            rows_h = slice(hd * NSA_DH, (hd + 1) * NSA_DH)
            e = jnp.exp(s.astype(BF16) - jnp.max(s, axis=0, keepdims=True).astype(BF16))
            ow = sum(_dot(vwt_ref[g, :, pl.ds(w_tile[j], _WT)], e[j * _WT:(j + 1) * _WT])
                     for j in range(n_wt))
            acc = acc_scr[g, :, hh * tq:(hh + 1) * tq]
            gr = GATE_COL0 + 3 * hd
            yt_scr[rows_h, :] = (gates_t[gr:gr + 1] * ocmp_ref[rows_h, :].astype(F32)
                                 + gates_t[gr + 1:gr + 2] * (acc[:NSA_DH] / acc[NSA_DH:NSA_DH + 1])
                                 + gates_t[gr + 2:gr + 3] * (ow[:NSA_DH] / ow[NSA_DH:NSA_DH + 1]))

    def general_band(s):
        kpos = w0 + lax.broadcasted_iota(jnp.int32, (wlen, 1), 0)
        return jnp.where((kpos <= t_q) & (kpos > t_q - WINDOW), s, NEG_BIG)

    if wlen == tq + WINDOW and WINDOW % tq == 0:
        r = lax.broadcasted_iota(jnp.int32, (tq, tq), 0)
        c = lax.broadcasted_iota(jnp.int32, (tq, tq), 1)

        def interior_band(s):
            return jnp.concatenate([jnp.where(r > c, s[:tq], NEG_BIG), s[tq:WINDOW],
                                    jnp.where(r <= c, s[WINDOW:], NEG_BIG)], axis=0)

        pl.when(q0 >= WINDOW)(functools.partial(window, interior_band))
        pl.when(q0 < WINDOW)(functools.partial(window, general_band))
    else:
        window(general_band)
    y_ref[...] = yt_scr[...].T.astype(y_ref.dtype)


def _attend(q_t, mask_t, ocmp_t, narrow, kaug, v_t, kw, vw_t, tq):
    B, _, S = q_t.shape
    G = NSA_KV_GROUPS
    rows = NSA_HPG * tq
    return pl.pallas_call(
        _attend_kernel,
        grid=(B, S // tq),
        in_specs=[
            pl.BlockSpec((None, N_Q, tq), lambda b, i: (b, 0, i)),
            pl.BlockSpec((None, G, SLC_SLOTS, tq), lambda b, i: (b, 0, 0, i)),
            pl.BlockSpec((None, N_Q, tq), lambda b, i: (b, 0, i)),
            pl.BlockSpec((None, tq, _NARROW_W), lambda b, i: (b, i, 0)),
            pl.BlockSpec((None, S, _KAUG_W), lambda b, i: (b, 0, 0)),
            pl.BlockSpec((None, G, _VT_ROWS, S), lambda b, i: (b, 0, 0, 0)),
            pl.BlockSpec((None, S, N_KV), lambda b, i: (b, 0, 0)),
            pl.BlockSpec((None, G, _VT_ROWS, S), lambda b, i: (b, 0, 0, 0)),
        ],
        out_specs=pl.BlockSpec((None, tq, N_Q), lambda b, i: (b, i, 0)),
        out_shape=jax.ShapeDtypeStruct((B, S, N_Q), BF16),
        scratch_shapes=[
            pltpu.VMEM((G, _KAUG_W, rows), BF16),
            pltpu.VMEM((G, 8, rows), F32),
            pltpu.VMEM((G, _VT_ROWS, rows), F32),
            pltpu.VMEM((N_Q, tq), F32),
        ],
        compiler_params=_cparams(("parallel", "arbitrary")),
        name="nsa_attend",
    )(q_t, mask_t, ocmp_t, narrow, kaug, v_t, kw, vw_t)


def _merge_kernel(x_ref, ya_ref, yb_ref, ga_ref, gb_ref, wa_ref, wb_ref, wo_ref, g2_ref, x1_ref, h2_ref):
    a = _sigmoid(ga_ref[...].astype(F32)) * _dot(ya_ref[...], wa_ref[...])
    b = _sigmoid(gb_ref[...].astype(F32)) * _dot(yb_ref[...], wb_ref[...])
    x1 = x_ref[...] + _dot((a + b).astype(BF16), wo_ref[...])
    x1_ref[...] = x1
    h2_ref[...] = _rms(x1, g2_ref[...]).astype(h2_ref.dtype)


def _merge(x2d, ya, yb, ga, gb, wa, wb, wo, g2, tm):
    T = x2d.shape[0]
    row = lambda w: pl.BlockSpec((tm, w), lambda i: (i, 0))
    const = lambda r, c: pl.BlockSpec((r, c), lambda i: (0, 0))
    return pl.pallas_call(
        _merge_kernel,
        grid=(T // tm,),
        in_specs=[row(D_MODEL), row(M_V), row(N_Q), row(D_MODEL), row(D_MODEL),
                  const(M_V, D_MODEL), const(N_Q, D_MODEL), const(D_MODEL, D_MODEL), const(1, D_MODEL)],
        out_specs=[row(D_MODEL), row(D_MODEL)],
        out_shape=[jax.ShapeDtypeStruct((T, D_MODEL), F32), jax.ShapeDtypeStruct((T, D_MODEL), BF16)],
        compiler_params=_cparams(("parallel",)),
        name="merge",
    )(x2d, ya, yb, ga, gb, wa.astype(BF16), wb.astype(BF16), wo.astype(BF16), g2[None, :])


_FFN_CHUNK = 256


def _ffn_kernel(x1_ref, h2_ref, wg_ref, wu_ref, wd_ref, gf_ref, out_ref):
    h = h2_ref[...]
    acc = x1_ref[...]
    for c in range(FFN_HIDDEN // _FFN_CHUNK):
        sl = slice(c * _FFN_CHUNK, (c + 1) * _FFN_CHUNK)
        gate = _dot(h, wg_ref[:, sl])
        up = _dot(h, wu_ref[:, sl])
        acc = acc + _dot((gate * _sigmoid(gate) * up).astype(BF16), wd_ref[sl, :])
    out_ref[...] = _rms(acc, gf_ref[...])


def _ffn(x1, h2, wg, wu, wd, gf, tm):
    T = x1.shape[0]
    row = lambda: pl.BlockSpec((tm, D_MODEL), lambda i: (i, 0))
    const = lambda r, c: pl.BlockSpec((r, c), lambda i: (0, 0))
    return pl.pallas_call(
        _ffn_kernel,
        grid=(T // tm,),
        in_specs=[row(), row(), const(D_MODEL, FFN_HIDDEN), const(D_MODEL, FFN_HIDDEN),
                  const(FFN_HIDDEN, D_MODEL), const(1, D_MODEL)],
        out_specs=row(),
        out_shape=jax.ShapeDtypeStruct((T, D_MODEL), F32),
        compiler_params=_cparams(("parallel",)),
        name="ffn",
    )(x1, h2, wg.astype(BF16), wu.astype(BF16), wd.astype(BF16), gf[None, :])


def _layer(x, norm1_g, w_in, b_in, f_bias, conv_w, conv_b, mlstm_norm_g, cmp_k_pos, cmp_k_w1, cmp_k_w2,
           cmp_v_pos, cmp_v_w1, cmp_v_w2, w_branch_a, w_branch_b, w_out, norm2_g):
    B, S, D = x.shape
    T = B * S
    tm = min(256, S)

    (m_q, m_k, m_v, m_o, n_q, gate_a, gate_b, narrow, narrow_t, cmp_kv, kaug, vs_t, k_win, vw_t) = _inproj(
        x, norm1_g, w_in, b_in, conv_w, conv_b, tm)
    narrow = narrow.reshape(B, S, _NARROW_W)

    y_a = _mlstm(m_q.reshape(B, S, M_QK), m_k.reshape(B, S, M_QK), m_v.reshape(B, S, M_V),
                 m_o.reshape(B, S, M_V), narrow, narrow_t, f_bias, mlstm_norm_g, min(512, S))

    cmp, cmp_t = _compress(cmp_kv.reshape(B, S, 2 * N_KV), jnp.stack([cmp_k_pos, cmp_v_pos]),
                           jnp.stack([cmp_k_w1, cmp_v_w1]), jnp.stack([cmp_k_w2, cmp_v_w2]))
    tq = min(256, S)
    q_t, ocmp_t, mask_t = _select(n_q.reshape(B, S, N_Q), cmp, cmp_t, tq)
    y_b = _attend(q_t, mask_t, ocmp_t, narrow, kaug.reshape(B, S, _KAUG_W), vs_t,
                  k_win.reshape(B, S, N_KV), vw_t, tq)

    return _merge(x.reshape(T, D), y_a.reshape(T, M_V), y_b.reshape(T, N_Q), gate_a, gate_b,
                  w_branch_a, w_branch_b, w_out, norm2_g, min(512, T))


def kernel(x, norm1_g, w_in, b_in, f_bias, conv_w, conv_b, mlstm_norm_g, cmp_k_pos, cmp_k_w1, cmp_k_w2,
           cmp_v_pos, cmp_v_w1, cmp_v_w2, w_branch_a, w_branch_b, w_out, norm2_g, w_ffn_gate, w_ffn_up,
           w_ffn_down, norm_f_g):
    B, S, D = x.shape
    depth = w_in.shape[0]
    assert depth == 1, "the fused final norm assumes a single layer"
    x1, h2 = _layer(x, norm1_g[0], w_in[0], b_in[0], f_bias[0], conv_w[0], conv_b[0], mlstm_norm_g[0],
                    cmp_k_pos[0], cmp_k_w1[0], cmp_k_w2[0], cmp_v_pos[0], cmp_v_w1[0], cmp_v_w2[0],
                    w_branch_a[0], w_branch_b[0], w_out[0], norm2_g[0])
    out = _ffn(x1, h2, w_ffn_gate[0], w_ffn_up[0], w_ffn_down[0], norm_f_g, min(512, B * S))
    return out.reshape(B, S, D)
```

```python
import functools

import numpy as np
import jax
import jax.numpy as jnp
from jax import lax
from jax.experimental import pallas as pl
from jax.experimental.pallas import tpu as pltpu

F32 = jnp.float32
BF16 = jnp.bfloat16

D_MODEL = 1024
MLSTM_HEADS = 4
MLSTM_DV = D_MODEL // MLSTM_HEADS
MLSTM_DQK = MLSTM_DV // 2
CONV_K = 4
NSA_DH = 64
NSA_HEADS = (D_MODEL // 2) // NSA_DH
NSA_KV_GROUPS = 2
NSA_HPG = NSA_HEADS // NSA_KV_GROUPS
CMP_BLOCK = 32
CMP_STRIDE = 16
CMP_HIDDEN = 256
SLC_BLOCK = 64
SLC_TOPN = 16
WINDOW = 512
FFN_HIDDEN = 2816
RMS_EPS = 1e-6
SEL_BIG = 1e9

M_QK = MLSTM_HEADS * MLSTM_DQK
M_V = MLSTM_HEADS * MLSTM_DV
N_Q = NSA_HEADS * NSA_DH
N_KV = NSA_KV_GROUPS * NSA_DH
IN_SPLITS = (2 * M_QK, M_V, M_V, MLSTM_HEADS, MLSTM_HEADS, N_Q, 6 * N_KV, 3 * NSA_HEADS, D_MODEL, D_MODEL)

LANES = 128
SLC_SLOTS = 128
MASK_BIG = float(2.0 ** 100)
NEG_BIG = -1e30
VMEM_LIMIT = 56 * 1024 * 1024
assert N_KV == LANES, "both KV groups are packed into one 128-lane key row"

_SEG = dict(zip(("m_qk", "m_v", "m_o", "m_i", "m_f", "n_q", "n_kv", "n_g", "gate_a", "gate_b"),
                zip(np.cumsum((0,) + IN_SPLITS[:-1]).tolist(), IN_SPLITS)))
_WIDE = ("m_qk", "m_v", "m_o", "n_q", "gate_a", "gate_b", "n_kv")
_NARROW = ("m_i", "m_f", "n_g")
_NARROW_W = LANES
GATE_COL0 = 2 * MLSTM_HEADS
_VT_ROWS = 80


def _cparams(sem):
    return pltpu.CompilerParams(dimension_semantics=sem, vmem_limit_bytes=VMEM_LIMIT)


def _sigmoid(x):
    return 1.0 / (1.0 + jnp.exp(-x))


def _log_sigmoid(x):
    return jnp.minimum(x, 0.0) - jnp.log(1.0 + jnp.exp(-jnp.abs(x)))


def _rms(x, g):
    ms = jnp.mean(x * x, axis=-1, keepdims=True)
    return x * lax.rsqrt(ms + RMS_EPS) * g


def _dot_nt(a, b):
    return lax.dot_general(a, b, (((1,), (1,)), ((), ())), preferred_element_type=F32)


def _dot(a, b):
    return jnp.dot(a, b, preferred_element_type=F32)


def _values_t(v):
    v_t = v.T
    row = lax.broadcasted_iota(jnp.int32, (_VT_ROWS - NSA_DH, v.shape[0]), 0)
    tail = jnp.where(row == 0, 1.0, 0.0)
    return [jnp.concatenate([v_t[g * NSA_DH:(g + 1) * NSA_DH], tail], axis=0).astype(BF16)
            for g in range(NSA_KV_GROUPS)]


_HALO = 8
_CONV_CHUNK = 2 * M_QK // 4
assert M_QK % _CONV_CHUNK == 0
_CONV_ROWS = 256


def _inproj_kernel(x_ref, g_ref, w_ref, b_ref, cw_ref, cb_ref, q_ref, k_ref, v_ref, o_ref, nq_ref, ga_ref,
                   gb_ref, narrow_ref, narrow_t_ref, cmp_ref, kaug_ref, vst_ref, kw_ref, vwt_ref, halo_scr,
                   *, offs, tiles_per_seq):
    tm = x_ref.shape[0]
    first_of_seq = pl.program_id(0) % tiles_per_seq == 0
    h = _rms(x_ref[...], g_ref[...]).astype(BF16)

    def seg(off, width):
        return _dot(h, w_ref[:, off:off + width]) + b_ref[:, off:off + width]

    def conv_chunk(c):
        cols = slice(c * _CONV_CHUNK, (c + 1) * _CONV_CHUNK)
        qk = seg(offs["m_qk"] + c * _CONV_CHUNK, _CONV_CHUNK)
        halo = jnp.where(first_of_seq, 0.0, halo_scr[:, cols])
        halo_scr[:, cols] = qk[tm - _HALO:]
        for r0 in range(0, tm, _CONV_ROWS):
            rows = slice(r0, r0 + _CONV_ROWS)
            piece = qk[rows]
            padded = jnp.concatenate([halo if r0 == 0 else qk[r0 - _HALO:r0], piece], axis=0)
            acc = piece * cw_ref[CONV_K - 1:CONV_K, cols] + cb_ref[:, cols]
            for j in range(1, CONV_K):
                acc = acc + pltpu.roll(padded, j, 0)[_HALO:] * cw_ref[CONV_K - 1 - j:CONV_K - j, cols]
            act = acc * _sigmoid(acc)
            if c * _CONV_CHUNK < M_QK:
                q_ref[rows, cols] = act.astype(BF16)
            else:
                k_ref[rows, c * _CONV_CHUNK - M_QK:(c + 1) * _CONV_CHUNK - M_QK] = (
                    act * (MLSTM_DQK ** -0.5)).astype(BF16)

    others = [(v_ref, offs["m_v"], M_V), (o_ref, offs["m_o"], M_V), (ga_ref, offs["gate_a"], D_MODEL),
              (gb_ref, offs["gate_b"], D_MODEL)]
    per = 2 * M_QK // _CONV_CHUNK // len(others)
    for n, (ref, off, width) in enumerate(others):
        for c in range(n * per, (n + 1) * per):
            conv_chunk(c)
        ref[...] = seg(off, width).astype(BF16)
    nq_ref[...] = seg(offs["n_q"], N_Q).astype(BF16)

    assert offs["narrow"] == offs["n_kv"] + 6 * N_KV
    kv = seg(offs["n_kv"], 6 * N_KV + _NARROW_W)
    part = lambda j, n=1: kv[:, j * N_KV:(j + n) * N_KV]
    narrow = kv[:, 6 * N_KV:]
    narrow_ref[...] = narrow
    narrow_t_ref[...] = narrow.T
    cmp_ref[...] = part(0, 2)
    pos = (pl.program_id(0) % tiles_per_seq) * tm + lax.broadcasted_iota(jnp.int32, (tm, SLC_SLOTS), 0)
    onehot = (pos // SLC_BLOCK == lax.broadcasted_iota(jnp.int32, (tm, SLC_SLOTS), 1)).astype(BF16)
    kaug_ref[...] = jnp.concatenate([onehot, part(2).astype(BF16)], axis=1)
    for g, vt in enumerate(_values_t(part(3))):
        vst_ref[g] = vt
    kw_ref[...] = part(4).astype(BF16)
    for g, vt in enumerate(_values_t(part(5))):
        vwt_ref[g] = vt


def _inproj(x, norm_g, w_in, b_in, conv_w, conv_b, tm):
    B, S, _ = x.shape
    T = B * S
    G = NSA_KV_GROUPS
    cols = [w_in[:, _SEG[n][0]:_SEG[n][0] + _SEG[n][1]] for n in _WIDE]
    bias = [b_in[_SEG[n][0]:_SEG[n][0] + _SEG[n][1]] for n in _WIDE]
    narrow_w = jnp.concatenate([w_in[:, _SEG[n][0]:_SEG[n][0] + _SEG[n][1]] for n in _NARROW], axis=1)
    narrow_b = jnp.concatenate([b_in[_SEG[n][0]:_SEG[n][0] + _SEG[n][1]] for n in _NARROW])
    pad = _NARROW_W - narrow_w.shape[1]
    cols.append(jnp.pad(narrow_w, ((0, 0), (0, pad))))
    bias.append(jnp.pad(narrow_b, (0, pad)))
    w = jnp.concatenate(cols, axis=1).astype(BF16)
    b = jnp.concatenate(bias)[None, :].astype(F32)
    widths = [_SEG[n][1] for n in _WIDE] + [_NARROW_W]
    offs = dict(zip(_WIDE + ("narrow",), np.cumsum([0] + widths[:-1]).tolist()))
    n_tot = w.shape[1]
    tps = S // tm

    def rows(width, dtype):
        return pl.BlockSpec((tm, width), lambda i: (i, 0)), jax.ShapeDtypeStruct((T, width), dtype)

    def rows_t(lead, dtype):
        nd = len(lead)
        return (pl.BlockSpec((None,) + lead + (tm,), lambda i: (i // tps,) + (0,) * nd + (i % tps,)),
                jax.ShapeDtypeStruct((B,) + lead + (S,), dtype))

    outs = [rows(M_QK, BF16), rows(M_QK, BF16), rows(M_V, BF16), rows(M_V, BF16), rows(N_Q, BF16),
            rows(D_MODEL, BF16), rows(D_MODEL, BF16), rows(_NARROW_W, F32), rows_t((_NARROW_W,), F32),
            rows(2 * N_KV, F32), rows(SLC_SLOTS + N_KV, BF16), rows_t((G, _VT_ROWS), BF16), rows(N_KV, BF16),
            rows_t((G, _VT_ROWS), BF16)]
    return pl.pallas_call(
        functools.partial(_inproj_kernel, offs=offs, tiles_per_seq=tps),
        grid=(T // tm,),
        in_specs=[
            pl.BlockSpec((tm, D_MODEL), lambda i: (i, 0)),
            pl.BlockSpec((1, D_MODEL), lambda i: (0, 0)),
            pl.BlockSpec((D_MODEL, n_tot), lambda i: (0, 0)),
            pl.BlockSpec((1, n_tot), lambda i: (0, 0)),
            pl.BlockSpec((CONV_K, 2 * M_QK), lambda i: (0, 0)),
            pl.BlockSpec((1, 2 * M_QK), lambda i: (0, 0)),
        ],
        out_specs=[o[0] for o in outs],
        out_shape=[o[1] for o in outs],
        scratch_shapes=[pltpu.VMEM((_HALO, 2 * M_QK), F32)],
        compiler_params=_cparams(("arbitrary",)),
        name="inproj",
    )(x.reshape(T, D_MODEL), norm_g[None, :], w, b, conv_w, conv_b[None, :])


def _mlstm_kernel(q_ref, k_ref, v_ref, o_ref, gr_ref, gc_ref, fb_ref, fbl_ref, ng_ref, y_ref, c_scr, m_scr):
    H, DK, DV = MLSTM_HEADS, MLSTM_DQK, MLSTM_DV
    L = q_ref.shape[1]
    chains = [(bi, h) for bi in range(q_ref.shape[0]) for h in range(H)]
    heads = range(len(chains))

    @pl.when(pl.program_id(1) == 0)
    def _():
        c_scr[...] = jnp.zeros_like(c_scr)
        m_scr[...] = jnp.zeros_like(m_scr)

    r_t = lax.broadcasted_iota(jnp.int32, (L, L), 0)
    r_s = lax.broadcasted_iota(jnp.int32, (L, L), 1)
    causal = r_s <= r_t
    upper = r_t <= r_s
    gates = [gc_ref[bi] for bi in range(q_ref.shape[0])]
    lf_cols = [_log_sigmoid(g + fbl_ref[...]) for g in gates]

    qs = [q_ref[bi, :, h * DK:(h + 1) * DK] for bi, h in chains]
    ks = [k_ref[bi, :, h * DK:(h + 1) * DK] for bi, h in chains]
    qk = [_dot_nt(qs[n], ks[n]) for n in heads]
    v_ext = [jnp.concatenate([v_ref[bi, :, h * DV:(h + 1) * DV], jnp.ones((L, LANES), BF16)], axis=1)
             for bi, h in chains]
    state = [c_scr[n] for n in heads]
    inter = [_dot(qs[n], state[n].astype(BF16)) for n in heads]

    stab = []
    for n, (bi, h) in enumerate(chains):
        fb = fb_ref[h:h + 1, 0:1]
        i_row = gr_ref[bi, h:h + 1, :]
        lf_row = _log_sigmoid(gr_ref[bi, H + h:H + h + 1, :] + fb)
        i_col = gates[bi][:, h:h + 1]
        lf_col = lf_cols[bi][:, H + h:H + h + 1]
        m_prev = m_scr[n]
        b_col = jnp.sum(jnp.where(causal, lf_row, 0.0), axis=1, keepdims=True)
        b_row = jnp.sum(jnp.where(upper, lf_col, 0.0), axis=0, keepdims=True)
        b_last = jnp.sum(lf_row, axis=1, keepdims=True)
        d_intra = jnp.where(causal, b_col - b_row + i_row, NEG_BIG)
        d_inter = b_col + m_prev
        m_t = jnp.maximum(d_inter, jnp.max(d_intra, axis=1, keepdims=True))
        d_state_row = b_last - b_row + i_row
        m_new = jnp.maximum(b_last + m_prev, jnp.max(d_state_row, axis=1, keepdims=True))
        stab.append(dict(w_intra=jnp.exp(d_intra - m_t), w_inter=jnp.exp(d_inter - m_t), floor=jnp.exp(-m_t),
                         w_state=jnp.exp(b_last - b_col + i_col - m_new),
                         decay=jnp.exp(b_last + m_prev - m_new), m_new=m_new))

    s = [(qk[n] * stab[n]["w_intra"]).astype(BF16) for n in heads]
    num = [_dot(s[n], v_ext[n]) + stab[n]["w_inter"] * inter[n] for n in heads]
    for n, (bi, h) in enumerate(chains):
        cols = slice(h * DV, (h + 1) * DV)
        hid = num[n][:, :DV] / jnp.maximum(jnp.abs(num[n][:, DV:DV + 1]), stab[n]["floor"])
        hid = _rms(hid, ng_ref[:, cols])
        y_ref[bi, :, cols] = (_sigmoid(o_ref[bi, :, cols].astype(F32)) * hid).astype(y_ref.dtype)
    k_t = [ks[n].astype(F32).T.astype(BF16) for n in heads]
    vw = [v_ext[n] * stab[n]["w_state"].astype(BF16) for n in heads]
    for n in heads:
        c_scr[n] = stab[n]["decay"] * state[n] + _dot(k_t[n], vw[n])
        m_scr[n] = stab[n]["m_new"]


def _mlstm(q, k, v, o, narrow, narrow_t, f_bias, norm_g, L):
    B, S, _ = q.shape
    H, DK, DV = MLSTM_HEADS, MLSTM_DQK, MLSTM_DV
    fb = jnp.broadcast_to(f_bias.astype(F32)[:, None], (H, LANES))
    fb_lanes = jnp.zeros((1, _NARROW_W), F32).at[0, H:2 * H].set(f_bias.astype(F32))
    nb = 2 if B % 2 == 0 else 1
    return pl.pallas_call(
        _mlstm_kernel,
        grid=(B // nb, S // L),
        in_specs=[
            pl.BlockSpec((nb, L, M_QK), lambda b, c: (b, c, 0)),
            pl.BlockSpec((nb, L, M_QK), lambda b, c: (b, c, 0)),
            pl.BlockSpec((nb, L, M_V), lambda b, c: (b, c, 0)),
            pl.BlockSpec((nb, L, M_V), lambda b, c: (b, c, 0)),
            pl.BlockSpec((nb, 2 * H, L), lambda b, c: (b, 0, c)),
            pl.BlockSpec((nb, L, _NARROW_W), lambda b, c: (b, c, 0)),
            pl.BlockSpec((H, LANES), lambda b, c: (0, 0)),
            pl.BlockSpec((1, _NARROW_W), lambda b, c: (0, 0)),
            pl.BlockSpec((1, M_V), lambda b, c: (0, 0)),
        ],
        out_specs=pl.BlockSpec((nb, L, M_V), lambda b, c: (b, c, 0)),
        out_shape=jax.ShapeDtypeStruct((B, S, M_V), BF16),
        scratch_shapes=[pltpu.VMEM((nb * H, DK, DV + LANES), F32), pltpu.VMEM((nb * H, 1, 1), F32)],
        compiler_params=_cparams(("parallel", "arbitrary")),
        name="mlstm",
    )(q, k, v, o, narrow_t, narrow, fb, fb_lanes, norm_g[None, :])


def _compress_kernel(x_ref, pos_ref, w1_ref, w2_ref, out_ref, out_t_ref, x_scr):
    S = x_ref.shape[0]
    n = S // CMP_STRIDE
    x_scr[0:S] = x_ref[...]
    x_scr[S:] = jnp.zeros((CMP_STRIDE, LANES), F32)
    acc = jnp.zeros((n, w1_ref.shape[2]), F32)
    for l in range(CMP_BLOCK):
        tok = x_scr[pl.ds(l, n, stride=CMP_STRIDE), :] + pos_ref[l:l + 1, :]
        acc = acc + _dot(tok.astype(BF16), w1_ref[l])
    hid = acc * _sigmoid(acc)
    out = _dot(hid.astype(BF16), w2_ref[...])
    row = lax.broadcasted_iota(jnp.int32, out.shape, 0)
    out = jnp.where(row < n - 1, out, 0.0)
    out_ref[...] = out.astype(out_ref.dtype)
    out_t_ref[...] = out.T.astype(out_t_ref.dtype)


def _block_diag2(w):
    z = jnp.zeros_like(w)
    return jnp.concatenate([jnp.concatenate([w, z], axis=-1), jnp.concatenate([z, w], axis=-1)], axis=-2)


def _compress(cmp_kv, pos, w1, w2):
    B, S, _ = cmp_kv.shape
    n = S // CMP_STRIDE
    pos2 = jnp.concatenate([pos, pos], axis=-1)
    w1b = _block_diag2(w1.reshape(2, CMP_BLOCK, NSA_DH, CMP_HIDDEN)).astype(BF16)
    w2b = _block_diag2(w2).astype(BF16)
    return pl.pallas_call(
        _compress_kernel,
        grid=(B, 2),
        in_specs=[
            pl.BlockSpec((None, S, LANES), lambda b, a: (b, 0, a)),
            pl.BlockSpec((None, CMP_BLOCK, LANES), lambda b, a: (a, 0, 0)),
            pl.BlockSpec((None, CMP_BLOCK, LANES, 2 * CMP_HIDDEN), lambda b, a: (a, 0, 0, 0)),
            pl.BlockSpec((None, 2 * CMP_HIDDEN, LANES), lambda b, a: (a, 0, 0)),
        ],
        out_specs=[pl.BlockSpec((None, None, n, LANES), lambda b, a: (b, a, 0, 0)),
                   pl.BlockSpec((None, None, LANES, n), lambda b, a: (b, a, 0, 0))],
        out_shape=[jax.ShapeDtypeStruct((B, 2, n, LANES), BF16),
                   jax.ShapeDtypeStruct((B, 2, LANES, n), BF16)],
        scratch_shapes=[pltpu.VMEM((S + CMP_STRIDE, LANES), F32)],
        compiler_params=_cparams(("parallel", "parallel")),
        name="nsa_compress",
    )(cmp_kv, pos2, w1b, w2b)


_CMP_PER_SLC = SLC_BLOCK // CMP_STRIDE
_FORCED = 3
_LOG2E = 1.4426950408889634


def _group_rows(q_h, g):
    z = jnp.zeros_like(q_h)
    return jnp.concatenate([q_h, z] if g == 0 else [z, q_h], axis=0)


def _select_kernel(q_ref, kc_ref, vct_ref, qt_ref, ocmp_ref, mask_ref, ps_scr):
    tq = q_ref.shape[0]
    t0 = pl.program_id(1) * tq
    q_f = q_ref[...].astype(F32).T * (NSA_DH ** -0.5)
    qt_ref[...] = q_f.astype(BF16)
    q_t = (q_f * _LOG2E).astype(BF16)
    need = (t0 + tq) // CMP_STRIDE
    for nk in range(LANES, kc_ref.shape[0] + 1, LANES):
        pl.when((need - 1) // LANES == nk // LANES - 1)(
            functools.partial(_select_body, q_t, kc_ref, vct_ref, ocmp_ref, mask_ref, ps_scr, t0, nk))


def _select_body(q_t, kc_ref, vct_ref, ocmp_ref, mask_ref, ps_scr, t0, ncp):
    tq = q_t.shape[1]
    n_slc = ncp // _CMP_PER_SLC
    t_row = t0 + lax.broadcasted_iota(jnp.int32, (1, tq), 1)
    n_vis = max(ncp - LANES - tq // CMP_STRIDE - 2 * CMP_BLOCK // CMP_STRIDE, 0) // 8 * 8
    c_col = n_vis + lax.broadcasted_iota(jnp.int32, (ncp - n_vis, 1), 0)
    visible = (c_col * CMP_STRIDE + (CMP_BLOCK - 1)) <= t_row
    any_visible = t_row >= CMP_BLOCK - 1

    blk = lax.broadcasted_iota(jnp.int32, (n_slc, tq), 0)
    cur = (t0 + lax.broadcasted_iota(jnp.int32, (n_slc, tq), 1)) // SLC_BLOCK
    forced = (blk == 0) | (blk == cur) | (blk == cur - 1)
    eligible = blk <= cur

    def scores(hd):
        return _dot(kc_ref[0:ncp, :], _group_rows(q_t[hd * NSA_DH:(hd + 1) * NSA_DH], hd // NSA_HPG))

    s_next = scores(0)
    for g in range(NSA_KV_GROUPS):
        vct = vct_ref[g * NSA_DH:(g + 1) * NSA_DH, 0:ncp]
        for hh in range(NSA_HPG):
            hd = g * NSA_HPG + hh
            s = jnp.where(visible, s_next[n_vis:], NEG_BIG)
            if n_vis:
                s = jnp.concatenate([s_next[:n_vis], s], axis=0)
            if hd + 1 < NSA_HEADS:
                s_next = scores(hd + 1)
            e = jnp.exp2(s - jnp.max(s, axis=0, keepdims=True))
            inv = jnp.where(any_visible, 1.0 / jnp.sum(e, axis=0, keepdims=True), 0.0)
            p = e * inv
            ocmp_ref[hd * NSA_DH:(hd + 1) * NSA_DH, :] = _dot(vct, p.astype(BF16)).astype(ocmp_ref.dtype)
            for c in range(tq // LANES):
                if hh == 0:
                    ps_scr[c, 0:ncp, :] = p[:, c * LANES:(c + 1) * LANES]
                else:
                    ps_scr[c, 0:ncp, :] += p[:, c * LANES:(c + 1) * LANES]
        lanes = [jnp.concatenate([ps_scr[c, pl.ds(r, n_slc, stride=_CMP_PER_SLC), :]
                                  for c in range(tq // LANES)], axis=1) for r in range(_CMP_PER_SLC)]
        row = lax.broadcasted_iota(jnp.int32, (n_slc, tq), 0)
        before = jnp.where(row == 0, 0.0, pltpu.roll(lanes[3], 1, 0))
        imp = before + 2.0 * (lanes[0] + lanes[1] + lanes[2]) + lanes[3]
        score = jnp.where(forced, -jnp.inf, jnp.where(eligible, imp, -SEL_BIG))
        for _ in range(SLC_TOPN - _FORCED):
            best = jnp.max(score, axis=0, keepdims=True)
            first = jnp.min(jnp.where(score == best, blk, SLC_SLOTS), axis=0, keepdims=True)
            score = jnp.where(blk == first, -jnp.inf, score)
        mask_ref[g, 0:n_slc, :] = jnp.where(score == -jnp.inf, 0.0, -MASK_BIG).astype(mask_ref.dtype)
        if n_slc < SLC_SLOTS:
            mask_ref[g, n_slc:, :] = jnp.full((SLC_SLOTS - n_slc, tq), -MASK_BIG, mask_ref.dtype)


def _select(n_q, cmp, cmp_t, tq):
    B, S, _ = n_q.shape
    G = NSA_KV_GROUPS
    ncp = cmp.shape[2]
    return pl.pallas_call(
        _select_kernel,
        grid=(B, S // tq),
        in_specs=[
            pl.BlockSpec((None, tq, N_Q), lambda b, i: (b, i, 0)),
            pl.BlockSpec((None, None, ncp, LANES), lambda b, i: (b, 0, 0, 0)),
            pl.BlockSpec((None, None, LANES, ncp), lambda b, i: (b, 1, 0, 0)),
        ],
        out_specs=[
            pl.BlockSpec((None, N_Q, tq), lambda b, i: (b, 0, i)),
            pl.BlockSpec((None, N_Q, tq), lambda b, i: (b, 0, i)),
            pl.BlockSpec((None, G, SLC_SLOTS, tq), lambda b, i: (b, 0, 0, i)),
        ],
        out_shape=[
            jax.ShapeDtypeStruct((B, N_Q, S), BF16),
            jax.ShapeDtypeStruct((B, N_Q, S), BF16),
            jax.ShapeDtypeStruct((B, G, SLC_SLOTS, S), BF16),
        ],
        scratch_shapes=[pltpu.VMEM((tq // LANES, ncp, LANES), F32)],
        compiler_params=_cparams(("parallel", "parallel")),
        name="nsa_select",
    )(n_q, cmp, cmp_t)


_TK = 256
_WT = 256
_KAUG_W = SLC_SLOTS + N_KV
_AHEAD = 7
_WINDOW_AHEAD = 3
_TILES_PER_TRIP = 8


def _pipelined(work, score, consume):
    pending = [score(item) for item in work[:_AHEAD]]
    for i, item in enumerate(work):
        s = pending.pop(0)
        if i + _AHEAD < len(work):
            pending.append(score(work[i + _AHEAD]))
        consume(item, s)


def _attend_kernel(qt_ref, mask_ref, ocmp_ref, gate_ref, kaug_ref, vt_ref, kw_ref, vwt_ref, y_ref,
                   qa_scr, m_scr, acc_scr, yt_scr):
    tq = qt_ref.shape[1]
    S = kaug_ref.shape[0]
    G = NSA_KV_GROUPS
    q0 = pl.program_id(1) * tq
    t_q = q0 + lax.broadcasted_iota(jnp.int32, (1, tq), 1)
    n_full = q0 // _TK
    kd = pl.multiple_of(n_full * _TK, _TK)
    causal = (kd + lax.broadcasted_iota(jnp.int32, (_TK, 1), 0)) <= t_q
    heads = [(g, hh) for g in range(G) for hh in range(NSA_HPG)]

    for g, hh in heads:
        hd = g * NSA_HPG + hh
        cols = slice(hh * tq, (hh + 1) * tq)
        qa_scr[g, 0:SLC_SLOTS, cols] = mask_ref[g]
        qa_scr[g, SLC_SLOTS:, cols] = _group_rows(qt_ref[hd * NSA_DH:(hd + 1) * NSA_DH, :], g)

    m_scr[...] = jnp.full(m_scr.shape, NEG_BIG, F32)
    acc_scr[...] = jnp.zeros_like(acc_scr)

    def selected_scores(item):
        (g, hh), k0 = item
        return _dot(kaug_ref[pl.ds(k0, _TK), :], qa_scr[g, :, hh * tq:(hh + 1) * tq])

    def selected_tiles(starts, diagonal):
        def consume(item, s):
            (g, hh), k0 = item
            cols = slice(hh * tq, (hh + 1) * tq)
            if diagonal:
                s = jnp.where(causal, s, -MASK_BIG)
            m_old = m_scr[g, :, cols]
            m_new = jnp.maximum(m_old, jnp.max(s, axis=0, keepdims=True).astype(BF16).astype(F32))
            p = jnp.exp(s.astype(BF16) - m_new[0:1].astype(BF16))
            acc_scr[g, :, cols] = (jnp.exp(m_old - m_new)[0:1] * acc_scr[g, :, cols]
                                   + _dot(vt_ref[g, :, pl.ds(k0, _TK)], p))
            m_scr[g, :, cols] = m_new
        _pipelined([(head, k0) for k0 in starts for head in heads], selected_scores, consume)

    def run_tiles(first, count):
        selected_tiles([pl.multiple_of(first + n * _TK, _TK) for n in range(count)], False)

    def stretch(j, carry):
        run_tiles(j * (_TILES_PER_TRIP * _TK), _TILES_PER_TRIP)
        return carry

    lax.fori_loop(0, n_full // _TILES_PER_TRIP, stretch, 0)
    count = _TILES_PER_TRIP // 2
    while count:
        pl.when(n_full & count != 0)(functools.partial(
            run_tiles, (n_full - n_full % (2 * count)) * _TK, count))
        count //= 2
    selected_tiles([kd], True)

    gates_t = _sigmoid(gate_ref[...]).T
    wlen = min(tq + WINDOW, S)
    w0 = pl.multiple_of(jnp.maximum(q0 - WINDOW, 0), tq)

    n_wt = wlen // _WT
    w_tile = [pl.multiple_of(w0 + j * _WT, _WT) for j in range(n_wt)]

    def window_scores(g, hh):
        q_cols = qa_scr[g, SLC_SLOTS:, hh * tq:(hh + 1) * tq]
        return [_dot(kw_ref[pl.ds(w_tile[j], _WT), :], q_cols) for j in range(n_wt)]

    def window(band_mask):
        pending = [window_scores(*head) for head in heads[:_WINDOW_AHEAD]]
        for i, (g, hh) in enumerate(heads):
            s = band_mask(jnp.concatenate(pending.pop(0), axis=0))
            if i + _WINDOW_AHEAD < len(heads):
                pending.append(window_scores(*heads[i + _WINDOW_AHEAD]))
            hd = g * NSA_HPG + hh
            rows_h = slice(hd * NSA_DH, (hd + 1) * NSA_DH)
            e = jnp.exp(s.astype(BF16) - jnp.max(s, axis=0, keepdims=True).astype(BF16))
            ow = sum(_dot(vwt_ref[g, :, pl.ds(w_tile[j], _WT)], e[j * _WT:(j + 1) * _WT])
                     for j in range(n_wt))
            acc = acc_scr[g, :, hh * tq:(hh + 1) * tq]
            gr = GATE_COL0 + 3 * hd
            yt_scr[rows_h, :] = (gates_t[gr:gr + 1] * ocmp_ref[rows_h, :].astype(F32)
                                 + gates_t[gr + 1:gr + 2] * (acc[:NSA_DH] / acc[NSA_DH:NSA_DH + 1])
                                 + gates_t[gr + 2:gr + 3] * (ow[:NSA_DH] / ow[NSA_DH:NSA_DH + 1]))

    def general_band(s):
        kpos = w0 + lax.broadcasted_iota(jnp.int32, (wlen, 1), 0)
        return jnp.where((kpos <= t_q) & (kpos > t_q - WINDOW), s, NEG_BIG)

    if wlen == tq + WINDOW and WINDOW % tq == 0:
        r = lax.broadcasted_iota(jnp.int32, (tq, tq), 0)
        c = lax.broadcasted_iota(jnp.int32, (tq, tq), 1)

        def interior_band(s):
            return jnp.concatenate([jnp.where(r > c, s[:tq], NEG_BIG), s[tq:WINDOW],
                                    jnp.where(r <= c, s[WINDOW:], NEG_BIG)], axis=0)

        pl.when(q0 >= WINDOW)(functools.partial(window, interior_band))
        pl.when(q0 < WINDOW)(functools.partial(window, general_band))
    else:
        window(general_band)
    y_ref[...] = yt_scr[...].T.astype(y_ref.dtype)


def _attend(q_t, mask_t, ocmp_t, narrow, kaug, v_t, kw, vw_t, tq):
    B, _, S = q_t.shape
    G = NSA_KV_GROUPS
    rows = NSA_HPG * tq
    return pl.pallas_call(
        _attend_kernel,
        grid=(B, S // tq),
        in_specs=[
            pl.BlockSpec((None, N_Q, tq), lambda b, i: (b, 0, i)),
            pl.BlockSpec((None, G, SLC_SLOTS, tq), lambda b, i: (b, 0, 0, i)),
            pl.BlockSpec((None, N_Q, tq), lambda b, i: (b, 0, i)),
            pl.BlockSpec((None, tq, _NARROW_W), lambda b, i: (b, i, 0)),
            pl.BlockSpec((None, S, _KAUG_W), lambda b, i: (b, 0, 0)),
            pl.BlockSpec((None, G, _VT_ROWS, S), lambda b, i: (b, 0, 0, 0)),
            pl.BlockSpec((None, S, N_KV), lambda b, i: (b, 0, 0)),
            pl.BlockSpec((None, G, _VT_ROWS, S), lambda b, i: (b, 0, 0, 0)),
        ],
        out_specs=pl.BlockSpec((None, tq, N_Q), lambda b, i: (b, i, 0)),
        out_shape=jax.ShapeDtypeStruct((B, S, N_Q), BF16),
        scratch_shapes=[
            pltpu.VMEM((G, _KAUG_W, rows), BF16),
            pltpu.VMEM((G, 8, rows), F32),
            pltpu.VMEM((G, _VT_ROWS, rows), F32),
            pltpu.VMEM((N_Q, tq), F32),
        ],
        compiler_params=_cparams(("parallel", "arbitrary")),
        name="nsa_attend",
    )(q_t, mask_t, ocmp_t, narrow, kaug, v_t, kw, vw_t)


def _merge_kernel(x_ref, ya_ref, yb_ref, ga_ref, gb_ref, wa_ref, wb_ref, wo_ref, g2_ref, x1_ref, h2_ref):
    a = _sigmoid(ga_ref[...].astype(F32)) * _dot(ya_ref[...], wa_ref[...])
    b = _sigmoid(gb_ref[...].astype(F32)) * _dot(yb_ref[...], wb_ref[...])
    x1 = x_ref[...] + _dot((a + b).astype(BF16), wo_ref[...])
    x1_ref[...] = x1
    h2_ref[...] = _rms(x1, g2_ref[...]).astype(h2_ref.dtype)


def _merge(x2d, ya, yb, ga, gb, wa, wb, wo, g2, tm):
    T = x2d.shape[0]
    row = lambda w: pl.BlockSpec((tm, w), lambda i: (i, 0))
    const = lambda r, c: pl.BlockSpec((r, c), lambda i: (0, 0))
    return pl.pallas_call(
        _merge_kernel,
        grid=(T // tm,),
        in_specs=[row(D_MODEL), row(M_V), row(N_Q), row(D_MODEL), row(D_MODEL),
                  const(M_V, D_MODEL), const(N_Q, D_MODEL), const(D_MODEL, D_MODEL), const(1, D_MODEL)],
        out_specs=[row(D_MODEL), row(D_MODEL)],
        out_shape=[jax.ShapeDtypeStruct((T, D_MODEL), F32), jax.ShapeDtypeStruct((T, D_MODEL), BF16)],
        compiler_params=_cparams(("parallel",)),
        name="merge",
    )(x2d, ya, yb, ga, gb, wa.astype(BF16), wb.astype(BF16), wo.astype(BF16), g2[None, :])


_FFN_CHUNK = 256


def _ffn_kernel(x1_ref, h2_ref, wg_ref, wu_ref, wd_ref, gf_ref, out_ref):
    h = h2_ref[...]
    acc = x1_ref[...]
    for c in range(FFN_HIDDEN // _FFN_CHUNK):
        sl = slice(c * _FFN_CHUNK, (c + 1) * _FFN_CHUNK)
        gate = _dot(h, wg_ref[:, sl])
        up = _dot(h, wu_ref[:, sl])
        acc = acc + _dot((gate * _sigmoid(gate) * up).astype(BF16), wd_ref[sl, :])
    out_ref[...] = _rms(acc, gf_ref[...])


def _ffn(x1, h2, wg, wu, wd, gf, tm):
    T = x1.shape[0]
    row = lambda: pl.BlockSpec((tm, D_MODEL), lambda i: (i, 0))
    const = lambda r, c: pl.BlockSpec((r, c), lambda i: (0, 0))
    return pl.pallas_call(
        _ffn_kernel,
        grid=(T // tm,),
        in_specs=[row(), row(), const(D_MODEL, FFN_HIDDEN), const(D_MODEL, FFN_HIDDEN),
                  const(FFN_HIDDEN, D_MODEL), const(1, D_MODEL)],
        out_specs=row(),
        out_shape=jax.ShapeDtypeStruct((T, D_MODEL), F32),
        compiler_params=_cparams(("parallel",)),
        name="ffn",
    )(x1, h2, wg.astype(BF16), wu.astype(BF16), wd.astype(BF16), gf[None, :])


def _layer(x, norm1_g, w_in, b_in, f_bias, conv_w, conv_b, mlstm_norm_g, cmp_k_pos, cmp_k_w1, cmp_k_w2,
           cmp_v_pos, cmp_v_w1, cmp_v_w2, w_branch_a, w_branch_b, w_out, norm2_g):
    B, S, D = x.shape
    T = B * S
    tm = min(256, S)

    (m_q, m_k, m_v, m_o, n_q, gate_a, gate_b, narrow, narrow_t, cmp_kv, kaug, vs_t, k_win, vw_t) = _inproj(
        x, norm1_g, w_in, b_in, conv_w, conv_b, tm)
    narrow = narrow.reshape(B, S, _NARROW_W)

    y_a = _mlstm(m_q.reshape(B, S, M_QK), m_k.reshape(B, S, M_QK), m_v.reshape(B, S, M_V),
                 m_o.reshape(B, S, M_V), narrow, narrow_t, f_bias, mlstm_norm_g, min(512, S))

    cmp, cmp_t = _compress(cmp_kv.reshape(B, S, 2 * N_KV), jnp.stack([cmp_k_pos, cmp_v_pos]),
                           jnp.stack([cmp_k_w1, cmp_v_w1]), jnp.stack([cmp_k_w2, cmp_v_w2]))
    tq = min(256, S)
    q_t, ocmp_t, mask_t = _select(n_q.reshape(B, S, N_Q), cmp, cmp_t, tq)
    y_b = _attend(q_t, mask_t, ocmp_t, narrow, kaug.reshape(B, S, _KAUG_W), vs_t,
                  k_win.reshape(B, S, N_KV), vw_t, tq)

    return _merge(x.reshape(T, D), y_a.reshape(T, M_V), y_b.reshape(T, N_Q), gate_a, gate_b,
                  w_branch_a, w_branch_b, w_out, norm2_g, min(512, T))


def kernel(x, norm1_g, w_in, b_in, f_bias, conv_w, conv_b, mlstm_norm_g, cmp_k_pos, cmp_k_w1, cmp_k_w2,
           cmp_v_pos, cmp_v_w1, cmp_v_w2, w_branch_a, w_branch_b, w_out, norm2_g, w_ffn_gate, w_ffn_up,
           w_ffn_down, norm_f_g):
    B, S, D = x.shape
    depth = w_in.shape[0]
    assert depth == 1, "the fused final norm assumes a single layer"
    x1, h2 = _layer(x, norm1_g[0], w_in[0], b_in[0], f_bias[0], conv_w[0], conv_b[0], mlstm_norm_g[0],
                    cmp_k_pos[0], cmp_k_w1[0], cmp_k_w2[0], cmp_v_pos[0], cmp_v_w1[0], cmp_v_w2[0],
                    w_branch_a[0], w_branch_b[0], w_out[0], norm2_g[0])
    out = _ffn(x1, h2, w_ffn_gate[0], w_ffn_up[0], w_ffn_down[0], norm_f_g, min(512, B * S))
    return out.reshape(B, S, D)
```
